```python
import math
import jax, jax.numpy as jnp
from jax import lax
import numpy as np

D_MODEL = 2048
BATCH = 2
SEQ = 4096
DEPTH = 2

SSM_WIDTH = 1024
SSM_GROUP = 16
SSM_GROUPS = SSM_WIDTH // SSM_GROUP
SSM_STATE = 64
DT_MIN = 1e-3
DT_MAX = 1e-1
HEAD_DIM = 64
N_Q_HEADS = 16
N_KV_HEADS = 4
Q_PER_KV = N_Q_HEADS // N_KV_HEADS
WINDOW = 128
ATTN_WIDTH = N_Q_HEADS * HEAD_DIM
KV_WIDTH = N_KV_HEADS * HEAD_DIM
SGU_WIDTH = 1024
SGU_HEADS = 8
SGU_HEAD_DIM = SGU_WIDTH // SGU_HEADS
SGU_CHUNK = 128
N_BRANCH = 3
BRANCH_WIDTH = 1024
PEER_HEADS = 8
PEER_QDIM = 256
PEER_HALF = PEER_QDIM // 2
N_KEYS = 128
N_EXPERTS = N_KEYS * N_KEYS
PEER_TOPK = 16
PEER_TOKEN_CHUNK = 128
EPS = 1e-6

SPLIT_POINTS = (
    SSM_WIDTH,
    SSM_WIDTH + ATTN_WIDTH,
    SSM_WIDTH + ATTN_WIDTH + KV_WIDTH,
    SSM_WIDTH + ATTN_WIDTH + 2 * KV_WIDTH,
    SSM_WIDTH + ATTN_WIDTH + 2 * KV_WIDTH + 2 * SGU_WIDTH,
)
IN_COLS = SPLIT_POINTS[-1] + N_BRANCH * D_MODEL

kernel_name = 'hybrid_s5_swa_sgu_peer_trunk'


def rms_norm(x, g):
    xf = x.astype(jnp.float32)
    y = xf * lax.rsqrt(jnp.mean(xf * xf, axis=-1, keepdims=True) + EPS)
    return (y * g.astype(jnp.float32)).astype(x.dtype)


def layer_norm(x, g, b):
    xf = x.astype(jnp.float32)
    mu = jnp.mean(xf, axis=-1, keepdims=True)
    xc = xf - mu
    y = xc * lax.rsqrt(jnp.mean(xc * xc, axis=-1, keepdims=True) + EPS)
    return (y * g.astype(jnp.float32) + b.astype(jnp.float32)).astype(x.dtype)


def s5_branch(u, a_re, a_im, b_re, b_im, c_re, c_im, d, log_dt, w_glu, b_glu):
    bsz, L = u.shape[0], u.shape[1]
    f32 = jnp.float32
    uf = u.astype(f32)
    ug = uf.reshape(bsz, L, SSM_GROUPS, SSM_GROUP)
    lam = lax.complex(a_re.astype(f32), a_im.astype(f32))
    dt = jnp.exp(log_dt.astype(f32))[:, None]
    a_bar = jnp.exp(lam * dt)
    b_c = lax.complex(b_re.astype(f32), b_im.astype(f32))
    b_bar = ((a_bar - 1.0) / lam)[..., None] * b_c
    bu = jnp.einsum('bsgh,gph->bsgp', ug, b_bar)
    a_seq = jnp.broadcast_to(a_bar[None, None], (1, L, SSM_GROUPS, SSM_STATE))

    def combine(e1, e2):
        a1, s1 = e1
        a2, s2 = e2
        return a2 * a1, a2 * s1 + s2

    _, states = lax.associative_scan(combine, (a_seq, bu), axis=1)
    y = (jnp.einsum('ghp,bsgp->bsgh', c_re.astype(f32), states.real)
         - jnp.einsum('ghp,bsgp->bsgh', c_im.astype(f32), states.imag))
    y = y.reshape(bsz, L, SSM_WIDTH) + d.astype(f32) * uf
    y = jax.nn.gelu(y, approximate=False)
    y = y * jax.nn.sigmoid(y @ w_glu.astype(f32) + b_glu.astype(f32))
    return y.astype(u.dtype)


def sliding_window_attention(q, k, v, q_g, k_g, sinks):
    bsz, L = q.shape[0], q.shape[1]
    nb = L // WINDOW
    f32 = jnp.float32
    q = rms_norm(q.reshape(bsz, L, N_KV_HEADS, Q_PER_KV, HEAD_DIM), q_g)
    k = rms_norm(k.reshape(bsz, L, N_KV_HEADS, HEAD_DIM), k_g)
    v = v.reshape(bsz, L, N_KV_HEADS, HEAD_DIM)
    qb = q.reshape(bsz, nb, WINDOW, N_KV_HEADS, Q_PER_KV, HEAD_DIM).astype(f32)
    kb = k.reshape(bsz, nb, WINDOW, N_KV_HEADS, HEAD_DIM).astype(f32)
    vb = v.reshape(bsz, nb, WINDOW, N_KV_HEADS, HEAD_DIM).astype(f32)
    pad = jnp.zeros_like(kb[:, :1])
    k_band = jnp.concatenate([jnp.concatenate([pad, kb[:, :-1]], axis=1), kb], axis=2)
    v_band = jnp.concatenate([jnp.concatenate([pad, vb[:, :-1]], axis=1), vb], axis=2)
    scores = jnp.einsum('bnqkgd,bnskd->bnkgqs', qb, k_band) * (HEAD_DIM ** -0.5)
    qpos = jnp.arange(WINDOW)[:, None] + WINDOW
    kpos = jnp.arange(2 * WINDOW)[None, :]
    diff = qpos - kpos
    valid = (diff >= 0) & (diff < WINDOW)
    first = jnp.arange(nb)[:, None, None] == 0
    mask = jnp.where(first, valid[None] & (kpos >= WINDOW)[None], valid[None])
    scores = jnp.where(mask[None, :, None, None], scores, -1e30)
    sink = jnp.broadcast_to(sinks.astype(f32).reshape(1, 1, N_KV_HEADS, Q_PER_KV, 1, 1),
                            scores.shape[:-1] + (1,))
    probs = jax.nn.softmax(jnp.concatenate([scores, sink], axis=-1), axis=-1)[..., :-1]
    out = jnp.einsum('bnkgqs,bnskd->bnqkgd', probs, v_band)
    return out.reshape(bsz, L, ATTN_WIDTH).astype(q.dtype)


def spatial_gating(z, ln_g, ln_b, w_s, b_s):
    bsz, L = z.shape[0], z.shape[1]
    nc = L // SGU_CHUNK
    z = jax.nn.gelu(z, approximate=False)
    u, v = jnp.split(z, 2, axis=-1)
    v = layer_norm(v, ln_g, ln_b)
    vb = v.reshape(bsz, nc, SGU_CHUNK, SGU_HEADS, SGU_HEAD_DIM)
    w = jnp.tril(w_s)
    mixed = jnp.einsum('hts,bnshc->bnthc', w, vb) + b_s.T[None, None, :, :, None]
    return u * mixed.reshape(bsz, L, SGU_WIDTH)


def peer_ffn(x, w_query, keys, peer_u, peer_v):
    bsz, L, D = x.shape
    q = (x @ w_query).reshape(bsz, L, PEER_HEADS, 2, PEER_HALF)
    s = jnp.einsum('blhcd,chkd->blhck', q, keys)
    sv, si = lax.top_k(s, PEER_TOPK)
    cand = (sv[..., 0, :, None] + sv[..., 1, None, :]).reshape(bsz, L, PEER_HEADS, PEER_TOPK * PEER_TOPK)
    cv, ci = lax.top_k(cand, PEER_TOPK)
    e1 = jnp.take_along_axis(si[..., 0, :], ci // PEER_TOPK, axis=-1)
    e2 = jnp.take_along_axis(si[..., 1, :], ci % PEER_TOPK, axis=-1)
    experts = e1 * N_KEYS + e2
    gates = jax.nn.softmax(cv.astype(jnp.float32), axis=-1)
    n_tok = bsz * L
    nchunk = n_tok // PEER_TOKEN_CHUNK
    xs = x.reshape(nchunk, PEER_TOKEN_CHUNK, D)
    es = experts.reshape(nchunk, PEER_TOKEN_CHUNK, PEER_HEADS * PEER_TOPK)
    gs = gates.reshape(nchunk, PEER_TOKEN_CHUNK, PEER_HEADS * PEER_TOPK)

    def chunk(args):
        xc, ec, gc = args
        u_sel = peer_u[ec]
        h = jnp.einsum('tkd,td->tk', u_sel, xc).astype(jnp.float32)
        a = (jax.nn.gelu(h, approximate=False) * gc).astype(x.dtype)
        v_sel = peer_v[ec]
        return jnp.einsum('tk,tkd->td', a, v_sel)

    out = lax.map(chunk, (xs, es, gs))
    return out.reshape(bsz, L, D).astype(x.dtype)


def setup_inputs(seed: int = 0) -> dict:
    key = jax.random.key(seed)
    ks = jax.random.split(key, 32)
    f32 = jnp.float32
    nrm = lambda k, shape, scale: jax.random.normal(k, shape, f32) * scale
    n_idx = jnp.arange(SSM_STATE, dtype=f32)
    return {
        'x': nrm(ks[0], (BATCH, SEQ, D_MODEL), 1.0),
        'norm1_g': 1.0 + nrm(ks[1], (DEPTH, D_MODEL), 0.02),
        'w_in': nrm(ks[2], (DEPTH, D_MODEL, IN_COLS), D_MODEL ** -0.5),
        'ssm_a_re': -0.5 + nrm(ks[3], (DEPTH, SSM_GROUPS, SSM_STATE), 0.01),
        'ssm_a_im': math.pi * n_idx + nrm(ks[4], (DEPTH, SSM_GROUPS, SSM_STATE), 0.01),
        'ssm_b_re': nrm(ks[5], (DEPTH, SSM_GROUPS, SSM_STATE, SSM_GROUP), (2 * SSM_GROUP) ** -0.5),
        'ssm_b_im': nrm(ks[6], (DEPTH, SSM_GROUPS, SSM_STATE, SSM_GROUP), (2 * SSM_GROUP) ** -0.5),
        'ssm_c_re': nrm(ks[7], (DEPTH, SSM_GROUPS, SSM_GROUP, SSM_STATE), SSM_STATE ** -0.5),
        'ssm_c_im': nrm(ks[8], (DEPTH, SSM_GROUPS, SSM_GROUP, SSM_STATE), SSM_STATE ** -0.5),
        'ssm_d': nrm(ks[9], (DEPTH, SSM_WIDTH), 1.0),
        'ssm_log_dt': jax.random.uniform(ks[10], (DEPTH, SSM_GROUPS), f32, math.log(DT_MIN), math.log(DT_MAX)),
        'w_glu': nrm(ks[11], (DEPTH, SSM_WIDTH, SSM_WIDTH), SSM_WIDTH ** -0.5),
        'b_glu': nrm(ks[12], (DEPTH, SSM_WIDTH), 0.01),
        'q_norm_g': 1.0 + nrm(ks[13], (DEPTH, HEAD_DIM), 0.02),
        'k_norm_g': 1.0 + nrm(ks[14], (DEPTH, HEAD_DIM), 0.02),
        'attn_sinks': nrm(ks[15], (DEPTH, N_Q_HEADS), 0.5),
        'sgu_ln_g': 1.0 + nrm(ks[16], (DEPTH, SGU_WIDTH), 0.02),
        'sgu_ln_b': nrm(ks[17], (DEPTH, SGU_WIDTH), 0.01),
        'sgu_w': nrm(ks[18], (DEPTH, SGU_HEADS, SGU_CHUNK, SGU_CHUNK), SGU_CHUNK ** -0.5),
        'sgu_b': 1.0 + nrm(ks[19], (DEPTH, SGU_HEADS, SGU_CHUNK), 0.02),
        'w_branch': nrm(ks[20], (DEPTH, N_BRANCH, BRANCH_WIDTH, D_MODEL), BRANCH_WIDTH ** -0.5),
        'w_out': nrm(ks[21], (DEPTH, D_MODEL, D_MODEL), 0.5 * D_MODEL ** -0.5),
        'norm2_g': 1.0 + nrm(ks[22], (DEPTH, D_MODEL), 0.02),
        'w_query': nrm(ks[23], (DEPTH, D_MODEL, PEER_HEADS * PEER_QDIM), D_MODEL ** -0.5),
        'peer_keys': nrm(ks[24], (DEPTH, 2, PEER_HEADS, N_KEYS, PEER_HALF), PEER_HALF ** -0.5),
        'peer_u': nrm(ks[25], (DEPTH, N_EXPERTS, D_MODEL), D_MODEL ** -0.5),
        'peer_v': nrm(ks[26], (DEPTH, N_EXPERTS, D_MODEL), 0.1),
    }


def reference(x, norm1_g, w_in, ssm_a_re, ssm_a_im, ssm_b_re, ssm_b_im, ssm_c_re, ssm_c_im,
              ssm_d, ssm_log_dt, w_glu, b_glu, q_norm_g, k_norm_g, attn_sinks,
              sgu_ln_g, sgu_ln_b, sgu_w, sgu_b, w_branch, w_out, norm2_g,
              w_query, peer_keys, peer_u, peer_v):
    bsz, L = x.shape[0], x.shape[1]
    for l in range(DEPTH):
        h = rms_norm(x, norm1_g[l])
        proj = h @ w_in[l]
        p_ssm, p_q, p_k, p_v, p_sgu, p_gate = jnp.split(proj, SPLIT_POINTS, axis=-1)
        y_a = s5_branch(p_ssm, ssm_a_re[l], ssm_a_im[l], ssm_b_re[l], ssm_b_im[l],
                        ssm_c_re[l], ssm_c_im[l], ssm_d[l], ssm_log_dt[l], w_glu[l], b_glu[l])
        y_b = sliding_window_attention(p_q, p_k, p_v, q_norm_g[l], k_norm_g[l], attn_sinks[l])
        y_c = spatial_gating(p_sgu, sgu_ln_g[l], sgu_ln_b[l], sgu_w[l], sgu_b[l])
        ys = jnp.stack([y_a, y_b, y_c], axis=2)
        branch = jnp.einsum('bsnc,ncd->bsnd', ys, w_branch[l])
        gates = jax.nn.sigmoid(p_gate.astype(jnp.float32)).reshape(bsz, L, N_BRANCH, D_MODEL)
        merged = jnp.sum(gates * branch.astype(jnp.float32), axis=2).astype(x.dtype)
        x = x + merged @ w_out[l]
        x = x + peer_ffn(rms_norm(x, norm2_g[l]), w_query[l], peer_keys[l], peer_u[l], peer_v[l])
    return x
```

```python
import functools
import math

import jax
import jax.numpy as jnp
from jax import lax
from jax.experimental import pallas as pl
from jax.experimental.pallas import tpu as pltpu

F32 = jnp.float32
BF16 = jnp.bfloat16

D_MODEL = 2048
DEPTH = 2
SSM_WIDTH = 1024
SSM_GROUP = 16
SSM_GROUPS = SSM_WIDTH // SSM_GROUP
SSM_STATE = 64
HEAD_DIM = 64
N_Q_HEADS = 16
N_KV_HEADS = 4
Q_PER_KV = N_Q_HEADS // N_KV_HEADS
WINDOW = 128
ATTN_WIDTH = N_Q_HEADS * HEAD_DIM
KV_WIDTH = N_KV_HEADS * HEAD_DIM
SGU_WIDTH = 1024
SGU_HEADS = 8
SGU_HEAD_DIM = SGU_WIDTH // SGU_HEADS
SGU_CHUNK = 128
N_BRANCH = 3
BRANCH_WIDTH = 1024
PEER_HEADS = 8
PEER_QDIM = 256
PEER_HALF = PEER_QDIM // 2
N_KEYS = 128
N_EXPERTS = N_KEYS * N_KEYS
PEER_TOPK = 16
EPS = 1e-6

V7X_VMEM_BYTES = 64 * 1024 * 1024
SUBLANES = 8
LANES = 128

COL_SSM = 0
COL_Q = COL_SSM + SSM_WIDTH
COL_SGU = COL_Q + ATTN_WIDTH
COL_GATE = COL_SGU + 2 * SGU_WIDTH
COL_K = COL_GATE + N_BRANCH * D_MODEL
COL_V = COL_K + KV_WIDTH
IN_COLS = COL_V + KV_WIDTH

S5_GB = 16
S5_NGB = SSM_GROUPS // S5_GB
S5_IN = S5_GB * SSM_GROUP
S5_ST = S5_GB * SSM_STATE
S5_CHUNK = 256

SQRT_HALF = math.sqrt(0.5)


def _vmem_limit(pipelined_bytes, resident_bytes):
    need = 2 * pipelined_bytes + resident_bytes + (4 << 20)
    return int(min(need, V7X_VMEM_BYTES - (8 << 20)))


def _gelu(x):
    return 0.5 * x * (1.0 + lax.erf(x * SQRT_HALF))


def _rms(x, g):
    return x * lax.rsqrt(jnp.mean(x * x, axis=-1, keepdims=True) + EPS) * g


def _matmul_kernel(*refs, norm, residual):
    refs = list(refs)
    a_ref = refs.pop(0)
    g_ref = refs.pop(0) if norm else None
    w_ref = refs.pop(0)
    r_ref = refs.pop(0) if residual else None
    o_ref, abf_ref = refs

    @pl.when(pl.program_id(1) == 0)
    def _prepare_lhs():
        a = a_ref[...]
        if norm:
            a = _rms(a, g_ref[...])
        abf_ref[...] = a.astype(BF16)

    acc = jnp.dot(abf_ref[...], w_ref[...], preferred_element_type=F32)
    if residual:
        acc = r_ref[...] + acc
    o_ref[...] = acc


def _matmul(a, w, *, gain=None, residual=None, tm, tn):
    m, k = a.shape
    n = w.shape[1]
    assert m % tm == 0 and n % tn == 0
    operands = [a]
    in_specs = [pl.BlockSpec((tm, k), lambda i, j: (i, 0))]
    if gain is not None:
        operands.append(gain.reshape(1, k))
        in_specs.append(pl.BlockSpec((1, k), lambda i, j: (0, 0)))
    operands.append(w)
    in_specs.append(pl.BlockSpec((k, tn), lambda i, j: (0, j)))
    if residual is not None:
        operands.append(residual)
        in_specs.append(pl.BlockSpec((tm, tn), lambda i, j: (i, j)))
    pipelined = tm * k * 4 + k * tn * 2 + tm * tn * 4 * (2 if residual is not None else 1)
    return pl.pallas_call(
        functools.partial(_matmul_kernel, norm=gain is not None, residual=residual is not None),
        grid=(m // tm, n // tn),
        in_specs=in_specs,
        out_specs=pl.BlockSpec((tm, tn), lambda i, j: (i, j)),
        out_shape=jax.ShapeDtypeStruct((m, n), F32),
        scratch_shapes=[pltpu.VMEM((tm, k), BF16)],
        compiler_params=pltpu.CompilerParams(
            dimension_semantics=("parallel", "arbitrary"),
            vmem_limit_bytes=_vmem_limit(pipelined, tm * k * 2 + tm * k * 4)),
        name="matmul",
    )(*operands)


def _s5_kernel(u_ref, d_ref, bm_ref, cm_ref, pw_ref, o_ref, x_ref, carry_ref):
    tc = u_ref.shape[0]
    ng = tc // SUBLANES

    @pl.when(pl.program_id(2) == 0)
    def _reset_state():
        carry_ref[...] = jnp.zeros_like(carry_ref)

    u = u_ref[...]
    bu = jnp.dot(u.astype(BF16), bm_ref[0], preferred_element_type=F32)
    re = bu[:, :S5_ST].reshape(ng, SUBLANES, S5_ST)
    im = bu[:, S5_ST:].reshape(ng, SUBLANES, S5_ST)
    for idx, k in enumerate((1, 2, 4)):
        lre = pw_ref[0, 2 * idx]
        lim = pw_ref[0, 2 * idx + 1]
        sre = pltpu.roll(re, k, 1)
        sim = pltpu.roll(im, k, 1)
        re, im = re + (lre * sre - lim * sim), im + (lre * sim + lim * sre)
    x_ref[:, :S5_ST] = re.reshape(tc, S5_ST)
    x_ref[:, S5_ST:] = im.reshape(tc, S5_ST)

    cyre = pw_ref[0, 6]
    cyim = pw_ref[0, 7]

    def _carry_group(g, carry):
        cr, ci = carry
        r0 = pl.multiple_of(g * SUBLANES, SUBLANES)
        xr = x_ref[pl.ds(r0, SUBLANES), :S5_ST] + (cyre * cr - cyim * ci)
        xi = x_ref[pl.ds(r0, SUBLANES), S5_ST:] + (cyre * ci + cyim * cr)
        x_ref[pl.ds(r0, SUBLANES), :S5_ST] = xr
        x_ref[pl.ds(r0, SUBLANES), S5_ST:] = xi
        last = SUBLANES - 1
        return (jnp.broadcast_to(xr[last:, :], (SUBLANES, S5_ST)),
                jnp.broadcast_to(xi[last:, :], (SUBLANES, S5_ST)))

    cr, ci = lax.fori_loop(0, ng, _carry_group, (carry_ref[0], carry_ref[1]))
    carry_ref[0] = cr
    carry_ref[1] = ci

    y = jnp.dot(x_ref[...].astype(BF16), cm_ref[0], preferred_element_type=F32)
    o_ref[...] = y + d_ref[...] * u


def _s5_discretise(a_re, a_im, b_re, b_im, c_re, c_im, log_dt):
    dt = jnp.exp(log_dt)[:, None]

    def a_pow(k):
        mag = jnp.exp(a_re * dt * k)
        return mag * jnp.cos(a_im * dt * k), mag * jnp.sin(a_im * dt * k)

    abar_re, abar_im = a_pow(1.0)
    num_re, num_im = abar_re - 1.0, abar_im
    den = a_re * a_re + a_im * a_im
    coef_re = (num_re * a_re + num_im * a_im) / den
    coef_im = (num_im * a_re - num_re * a_im) / den
    bbar_re = coef_re[..., None] * b_re - coef_im[..., None] * b_im
    bbar_im = coef_re[..., None] * b_im + coef_im[..., None] * b_re

    eye = jnp.eye(S5_GB, dtype=F32)

    def in_block(b):
        b = b.reshape(S5_NGB, S5_GB, SSM_STATE, SSM_GROUP)
        return jnp.einsum('bgph,gk->bghkp', b, eye).reshape(S5_NGB, S5_IN, S5_ST)

    def out_block(c):
        c = c.reshape(S5_NGB, S5_GB, SSM_GROUP, SSM_STATE)
        return jnp.einsum('bghp,gk->bgpkh', c, eye).reshape(S5_NGB, S5_ST, S5_IN)

    bmat = jnp.concatenate([in_block(bbar_re), in_block(bbar_im)], axis=2).astype(BF16)
    cmat = jnp.concatenate([out_block(c_re), out_block(-c_im)], axis=1).astype(BF16)

    rows = jnp.arange(SUBLANES)[:, None, None]

    def tile(vals):
        return vals.reshape(SUBLANES, S5_NGB, S5_ST).transpose(1, 0, 2)

    tiles = []
    for k in (1, 2, 4):
        pr, pi = a_pow(float(k))
        tiles.append(tile(jnp.where(rows >= k, pr[None], 0.0)))
        tiles.append(tile(jnp.where(rows >= k, pi[None], 0.0)))
    steps = (rows + 1).astype(F32)
    mag = jnp.exp(a_re[None] * dt[None] * steps)
    tiles.append(tile(mag * jnp.cos(a_im[None] * dt[None] * steps)))
    tiles.append(tile(mag * jnp.sin(a_im[None] * dt[None] * steps)))
    pw = jnp.stack(tiles, axis=1)
    return bmat, cmat, pw


def _s5_scan(proj, ssm_d, bmat, cmat, pw, *, batch, seq):
    n = batch * seq
    nc = seq // S5_CHUNK
    pipelined = S5_CHUNK * S5_IN * 4 * 2 + S5_IN * 2 * S5_ST * 2 * 2 + 8 * SUBLANES * S5_ST * 4
    resident = S5_CHUNK * 2 * S5_ST * 4 * 6
    return pl.pallas_call(
        _s5_kernel,
        grid=(S5_NGB, batch, nc),
        in_specs=[
            pl.BlockSpec((S5_CHUNK, S5_IN), lambda g, b, c: (b * nc + c, COL_SSM // S5_IN + g)),
            pl.BlockSpec((1, S5_IN), lambda g, b, c: (0, g)),
            pl.BlockSpec((1, S5_IN, 2 * S5_ST), lambda g, b, c: (g, 0, 0)),
            pl.BlockSpec((1, 2 * S5_ST, S5_IN), lambda g, b, c: (g, 0, 0)),
            pl.BlockSpec((1, 8, SUBLANES, S5_ST), lambda g, b, c: (g, 0, 0, 0)),
        ],
        out_specs=pl.BlockSpec((S5_CHUNK, S5_IN), lambda g, b, c: (b * nc + c, g)),
        out_shape=jax.ShapeDtypeStruct((n, SSM_WIDTH), F32),
        scratch_shapes=[pltpu.VMEM((S5_CHUNK, 2 * S5_ST), F32),
                        pltpu.VMEM((2, SUBLANES, S5_ST), F32)],
        compiler_params=pltpu.CompilerParams(
            dimension_semantics=("parallel", "parallel", "arbitrary"),
            vmem_limit_bytes=_vmem_limit(pipelined, resident)),
        name="s5_scan",
    )(proj, ssm_d.reshape(1, SSM_WIDTH), bmat, cmat, pw)


def _s5_glu_kernel(y_ref, w_ref, b_ref, o_ref):
    y = _gelu(y_ref[...])
    z = jnp.dot(y.astype(BF16), w_ref[...], preferred_element_type=F32) + b_ref[...]
    o_ref[...] = y * jax.nn.sigmoid(z)


def _s5_glu(y, w_glu, b_glu, *, tm):
    n = y.shape[0]
    return pl.pallas_call(
        _s5_glu_kernel,
        grid=(n // tm,),
        in_specs=[pl.BlockSpec((tm, SSM_WIDTH), lambda i: (i, 0)),
                  pl.BlockSpec((SSM_WIDTH, SSM_WIDTH), lambda i: (0, 0)),
                  pl.BlockSpec((1, SSM_WIDTH), lambda i: (0, 0))],
        out_specs=pl.BlockSpec((tm, SSM_WIDTH), lambda i: (i, 0)),
        out_shape=jax.ShapeDtypeStruct((n, SSM_WIDTH), F32),
        compiler_params=pltpu.CompilerParams(
            dimension_semantics=("parallel",),
            vmem_limit_bytes=_vmem_limit(2 * tm * SSM_WIDTH * 4 + SSM_WIDTH * SSM_WIDTH * 2,
                                         4 * tm * SSM_WIDTH * 4)),
        name="s5_glu",
    )(y, w_glu, b_glu.reshape(1, SSM_WIDTH))


def _swa_kernel(sink_ref, q_ref, kp_ref, kc_ref, vp_ref, vc_ref, qg_ref, kg_ref, o_ref, *, blocks_per_seq):
    first = (pl.program_id(0) % blocks_per_seq) == 0
    q = q_ref[...]
    k = jnp.concatenate([kp_ref[...], kc_ref[...]], axis=0)
    v = jnp.concatenate([vp_ref[...], vc_ref[...]], axis=0)
    qpos = lax.broadcasted_iota(jnp.int32, (WINDOW, 2 * WINDOW), 0) + WINDOW
    kpos = lax.broadcasted_iota(jnp.int32, (WINDOW, 2 * WINDOW), 1)
    diff = qpos - kpos
    first_key = jnp.where(first, WINDOW, 0)
    valid = (diff >= 0) & (diff < WINDOW) & (kpos >= first_key)
    outs = []
    for kv in range(N_KV_HEADS):
        ks = slice(kv * HEAD_DIM, (kv + 1) * HEAD_DIM)
        k_h = _rms(k[:, ks], kg_ref[...]).astype(BF16)
        v_h = v[:, ks].astype(BF16)
        for g in range(Q_PER_KV):
            h = kv * Q_PER_KV + g
            q_h = _rms(q[:, h * HEAD_DIM:(h + 1) * HEAD_DIM], qg_ref[...]).astype(BF16)
            s = lax.dot_general(q_h, k_h, (((1,), (1,)), ((), ())), preferred_element_type=F32)
            s = jnp.where(valid, s * (HEAD_DIM ** -0.5), -1e30)
            sink = sink_ref[h]
            m = jnp.maximum(jnp.max(s, axis=-1, keepdims=True), sink)
            p = jnp.exp(s - m)
            denom = jnp.sum(p, axis=-1, keepdims=True) + jnp.exp(sink - m)
            o_h = jnp.dot(p.astype(BF16), v_h, preferred_element_type=F32)
            outs.append(o_h / denom)
    o_ref[...] = jnp.concatenate(outs, axis=-1)


def _swa(proj, q_g, k_g, sinks, *, batch, seq):
    n = batch * seq
    nb = seq // WINDOW
    prev = lambda r: jnp.maximum(r - 1, 0)
    pipelined = WINDOW * (2 * ATTN_WIDTH + 4 * KV_WIDTH) * 4
    return pl.pallas_call(
        functools.partial(_swa_kernel, blocks_per_seq=nb),
        grid=(n // WINDOW,),
        in_specs=[
            pl.BlockSpec(memory_space=pltpu.SMEM),
            pl.BlockSpec((WINDOW, ATTN_WIDTH), lambda r: (r, COL_Q // ATTN_WIDTH)),
            pl.BlockSpec((WINDOW, KV_WIDTH), lambda r: (prev(r), COL_K // KV_WIDTH)),
            pl.BlockSpec((WINDOW, KV_WIDTH), lambda r: (r, COL_K // KV_WIDTH)),
            pl.BlockSpec((WINDOW, KV_WIDTH), lambda r: (prev(r), COL_V // KV_WIDTH)),
            pl.BlockSpec((WINDOW, KV_WIDTH), lambda r: (r, COL_V // KV_WIDTH)),
            pl.BlockSpec((1, HEAD_DIM), lambda r: (0, 0)),
            pl.BlockSpec((1, HEAD_DIM), lambda r: (0, 0)),
        ],
        out_specs=pl.BlockSpec((WINDOW, ATTN_WIDTH), lambda r: (r, 0)),
        out_shape=jax.ShapeDtypeStruct((n, ATTN_WIDTH), F32),
        compiler_params=pltpu.CompilerParams(
            dimension_semantics=("parallel",),
            vmem_limit_bytes=_vmem_limit(pipelined, 16 << 20)),
        name="swa",
    )(sinks, proj, proj, proj, proj, proj, q_g.reshape(1, HEAD_DIM), k_g.reshape(1, HEAD_DIM))


def _sgu_kernel(z_ref, g_ref, b_ref, w_ref, bs_ref, o_ref):
    tm = z_ref.shape[0]
    z = _gelu(z_ref[...])
    u = z[:, :SGU_WIDTH]
    v = z[:, SGU_WIDTH:]
    vc = v - jnp.mean(v, axis=-1, keepdims=True)
    vn = vc * lax.rsqrt(jnp.mean(vc * vc, axis=-1, keepdims=True) + EPS) * g_ref[...] + b_ref[...]
    vn = vn.astype(BF16)
    row = lax.broadcasted_iota(jnp.int32, (SGU_CHUNK, SGU_CHUNK), 0)
    col = lax.broadcasted_iota(jnp.int32, (SGU_CHUNK, SGU_CHUNK), 1)
    for h in range(SGU_HEADS):
        w_h = jnp.where(row >= col, w_ref[h], 0.0).astype(BF16)
        cs = slice(h * SGU_HEAD_DIM, (h + 1) * SGU_HEAD_DIM)
        for c in range(tm // SGU_CHUNK):
            rs = slice(c * SGU_CHUNK, (c + 1) * SGU_CHUNK)
            mixed = jnp.dot(w_h, vn[rs, cs], preferred_element_type=F32) + bs_ref[:, h:h + 1]
            o_ref[rs, cs] = u[rs, cs] * mixed


def _sgu(proj, ln_g, ln_b, w_s, b_s, *, tm):
    n = proj.shape[0]
    return pl.pallas_call(
        _sgu_kernel,
        grid=(n // tm,),
        in_specs=[
            pl.BlockSpec((tm, 2 * SGU_WIDTH), lambda i: (i, COL_SGU // (2 * SGU_WIDTH))),
            pl.BlockSpec((1, SGU_WIDTH), lambda i: (0, 0)),
            pl.BlockSpec((1, SGU_WIDTH), lambda i: (0, 0)),
            pl.BlockSpec((SGU_HEADS, SGU_CHUNK, SGU_CHUNK), lambda i: (0, 0, 0)),
            pl.BlockSpec((SGU_CHUNK, SGU_HEADS), lambda i: (0, 0)),
        ],
        out_specs=pl.BlockSpec((tm, SGU_WIDTH), lambda i: (i, 0)),
        out_shape=jax.ShapeDtypeStruct((n, SGU_WIDTH), F32),
        compiler_params=pltpu.CompilerParams(
            dimension_semantics=("parallel",),
            vmem_limit_bytes=_vmem_limit(tm * 3 * SGU_WIDTH * 4, 6 * tm * 2 * SGU_WIDTH * 4)),
        name="sgu",
    )(proj, ln_g.reshape(1, SGU_WIDTH), ln_b.reshape(1, SGU_WIDTH), w_s, b_s.T)


def _merge_kernel(ya_ref, yb_ref, yc_ref, w_ref, ga_ref, gb_ref, gc_ref, o_ref, ybf_ref):
    @pl.when(pl.program_id(1) == 0)
    def _cast_branches():
        for n, y_ref in enumerate((ya_ref, yb_ref, yc_ref)):
            ybf_ref[n] = y_ref[...].astype(BF16)

    acc = None
    for n, gate_ref in enumerate((ga_ref, gb_ref, gc_ref)):
        branch = jnp.dot(ybf_ref[n], w_ref[n], preferred_element_type=F32)
        term = jax.nn.sigmoid(gate_ref[...]) * branch
        acc = term if acc is None else acc + term
    o_ref[...] = acc


def _merge(ya, yb, yc, w_branch, proj, *, tm, tn):
    n = ya.shape[0]
    y_spec = pl.BlockSpec((tm, BRANCH_WIDTH), lambda i, j: (i, 0))

    def gate_spec(b):
        return pl.BlockSpec((tm, tn), lambda i, j: (i, (COL_GATE + b * D_MODEL) // tn + j))

    pipelined = 3 * tm * BRANCH_WIDTH * 4 + 3 * BRANCH_WIDTH * tn * 2 + 4 * tm * tn * 4
    return pl.pallas_call(
        _merge_kernel,
        grid=(n // tm, D_MODEL // tn),
        in_specs=[y_spec, y_spec, y_spec,
                  pl.BlockSpec((N_BRANCH, BRANCH_WIDTH, tn), lambda i, j: (0, 0, j)),
                  gate_spec(0), gate_spec(1), gate_spec(2)],
        out_specs=pl.BlockSpec((tm, tn), lambda i, j: (i, j)),
        out_shape=jax.ShapeDtypeStruct((n, D_MODEL), F32),
        scratch_shapes=[pltpu.VMEM((N_BRANCH, tm, BRANCH_WIDTH), BF16)],
        compiler_params=pltpu.CompilerParams(
            dimension_semantics=("parallel", "arbitrary"),
            vmem_limit_bytes=_vmem_limit(pipelined, 3 * tm * BRANCH_WIDTH * 2 + 4 * tm * tn * 4)),
        name="merge",
    )(ya, yb, yc, w_branch, proj, proj, proj)


def _top_values(s, count):
    rows = []
    for _ in range(count):
        m = jnp.max(s, axis=0, keepdims=True)
        rows.append(m)
        s = jnp.where(s == m, -jnp.inf, s)
    return jnp.concatenate(rows, axis=0)


def _peer_select_kernel(q_ref, keys_ref, s1_ref, w1_ref, s2_ref, w2_ref, thr_ref):
    nt = lax.dot_general
    contract_last = (((1,), (1,)), ((), ()))
    for h in range(PEER_HEADS):
        c0 = h * PEER_QDIM
        q1 = q_ref[:, c0:c0 + PEER_HALF].astype(BF16)
        q2 = q_ref[:, c0 + PEER_HALF:c0 + PEER_QDIM].astype(BF16)
        s1 = nt(keys_ref[0, h], q1, contract_last, preferred_element_type=F32)
        s2 = nt(keys_ref[1, h], q2, contract_last, preferred_element_type=F32)
        a = _top_values(s1, PEER_TOPK)
        b = _top_values(s2, PEER_TOPK)
        cand = [a[0:1] + b]
        cand += [a[p:p + 1] + b[0:SUBLANES] for p in range(1, SUBLANES)]
        cand.append(a[SUBLANES:] + b[0:1])
        cand = jnp.concatenate(cand, axis=0)
        work = cand
        for _ in range(PEER_TOPK - 1):
            m = jnp.max(work, axis=0, keepdims=True)
            work = jnp.where(work == m, -jnp.inf, work)
        thr = jnp.max(work, axis=0, keepdims=True)
        top = a[0:1] + b[0:1]
        z = jnp.sum(jnp.where(cand >= thr, jnp.exp(cand - top), 0.0), axis=0, keepdims=True)
        s1_ref[h] = s1
        s2_ref[h] = s2
        w1_ref[h] = jnp.exp(s1 - a[0:1])
        w2_ref[h] = jnp.exp(s2 - b[0:1]) / z
        thr_ref[h:h + 1, :] = thr


def _peer_select(q, keys, *, tm):
    n = q.shape[0]
    big = jax.ShapeDtypeStruct((PEER_HEADS, N_KEYS, n), F32)
    big_spec = pl.BlockSpec((PEER_HEADS, N_KEYS, tm), lambda i: (0, 0, i))
    return pl.pallas_call(
        _peer_select_kernel,
        grid=(n // tm,),
        in_specs=[pl.BlockSpec((tm, PEER_HEADS * PEER_QDIM), lambda i: (i, 0)),
                  pl.BlockSpec((2, PEER_HEADS, N_KEYS, PEER_HALF), lambda i: (0, 0, 0, 0))],
        out_specs=[big_spec, big_spec, big_spec, big_spec,
                   pl.BlockSpec((PEER_HEADS, tm), lambda i: (0, i))],
        out_shape=[big, big, big, big, jax.ShapeDtypeStruct((PEER_HEADS, n), F32)],
        compiler_params=pltpu.CompilerParams(
            dimension_semantics=("parallel",),
            vmem_limit_bytes=_vmem_limit(tm * PEER_HEADS * PEER_QDIM * 4 + 4 * PEER_HEADS * N_KEYS * tm * 4,
                                         16 << 20)),
        name="peer_select",
    )(q, keys)


def _peer_ffn_kernel(x_ref, g_ref, u_ref, vt_ref, s1_ref, w1_ref, s2_ref, w2_ref, thr_ref, o_ref,
                     xn_ref, acc_ref, h_ref, a_ref):
    te, tm = h_ref.shape
    e = pl.program_id(1)

    @pl.when(e == 0)
    def _start_token_tile():
        xn_ref[...] = _rms(x_ref[...], g_ref[...]).astype(BF16)
        acc_ref[...] = jnp.zeros_like(acc_ref)

    h_ref[...] = lax.dot_general(u_ref[...], xn_ref[...], (((1,), (1,)), ((), ())),
                                 preferred_element_type=F32)

    def _slab(i, _):
        r0 = pl.multiple_of(i * N_KEYS, N_KEYS)
        for c in range(tm // LANES):
            ls = slice(c * LANES, (c + 1) * LANES)
            gate = jnp.zeros((N_KEYS, LANES), F32)
            for hd in range(PEER_HEADS):
                s1 = s1_ref[hd, i, :, ls]
                w1 = w1_ref[hd, i, :, ls]
                hit = (s1 + s2_ref[hd, :, ls]) >= thr_ref[hd:hd + 1, ls]
                gate = gate + jnp.where(hit, w2_ref[hd, :, ls], 0.0) * w1
            a_ref[pl.ds(r0, N_KEYS), ls] = (_gelu(h_ref[pl.ds(r0, N_KEYS), ls]) * gate).astype(BF16)
        return 0

    lax.fori_loop(0, te // N_KEYS, _slab, 0)
    acc_ref[...] += jnp.dot(vt_ref[...], a_ref[...], preferred_element_type=F32)

    @pl.when(e == pl.num_programs(1) - 1)
    def _finish_token_tile():
        o_ref[...] = x_ref[...] + acc_ref[...].T


def _peer_ffn(x, gain, u_bf, vt_bf, s1, w1, s2, w2, thr, *, tm, te):
    n = x.shape[0]
    slabs = te // N_KEYS
    row_spec = pl.BlockSpec((PEER_HEADS, slabs, 1, tm), lambda t, e: (0, e, 0, t))
    s1 = s1.reshape(PEER_HEADS, N_KEYS, 1, n)
    w1 = w1.reshape(PEER_HEADS, N_KEYS, 1, n)
    full_spec = pl.BlockSpec((PEER_HEADS, N_KEYS, tm), lambda t, e: (0, 0, t))
    pipelined = (2 * tm * D_MODEL * 4 + 2 * te * D_MODEL * 2
                 + 2 * PEER_HEADS * (slabs * SUBLANES + N_KEYS) * tm * 4 + PEER_HEADS * tm * 4)
    resident = tm * D_MODEL * 2 + D_MODEL * tm * 4 + te * tm * 6 + D_MODEL * tm * 4
    return pl.pallas_call(
        _peer_ffn_kernel,
        grid=(n // tm, N_EXPERTS // te),
        in_specs=[
            pl.BlockSpec((tm, D_MODEL), lambda t, e: (t, 0)),
            pl.BlockSpec((1, D_MODEL), lambda t, e: (0, 0)),
            pl.BlockSpec((te, D_MODEL), lambda t, e: (e, 0)),
            pl.BlockSpec((D_MODEL, te), lambda t, e: (0, e)),
            row_spec, row_spec, full_spec, full_spec,
            pl.BlockSpec((PEER_HEADS, tm), lambda t, e: (0, t)),
        ],
        out_specs=pl.BlockSpec((tm, D_MODEL), lambda t, e: (t, 0)),
        out_shape=jax.ShapeDtypeStruct((n, D_MODEL), F32),
        scratch_shapes=[pltpu.VMEM((tm, D_MODEL), BF16),
                        pltpu.VMEM((D_MODEL, tm), F32),
                        pltpu.VMEM((te, tm), F32),
                        pltpu.VMEM((te, tm), BF16)],
        compiler_params=pltpu.CompilerParams(
            dimension_semantics=("parallel", "arbitrary"),
            vmem_limit_bytes=_vmem_limit(pipelined, resident)),
        name="peer_ffn",
    )(x, gain.reshape(1, D_MODEL), u_bf, vt_bf, s1, w1, s2, w2, thr)


def _permute_in_proj(w):
    ssm_q = SSM_WIDTH + ATTN_WIDTH
    kv_end = ssm_q + 2 * KV_WIDTH
    sgu_end = kv_end + 2 * SGU_WIDTH
    return jnp.concatenate([w[:, :ssm_q], w[:, kv_end:sgu_end], w[:, sgu_end:], w[:, ssm_q:kv_end]], axis=1)


def kernel(x, norm1_g, w_in, ssm_a_re, ssm_a_im, ssm_b_re, ssm_b_im, ssm_c_re, ssm_c_im, ssm_d, ssm_log_dt,
           w_glu, b_glu, q_norm_g, k_norm_g, attn_sinks, sgu_ln_g, sgu_ln_b, sgu_w, sgu_b, w_branch, w_out,
           norm2_g, w_query, peer_keys, peer_u, peer_v):
    batch, seq, d_model = x.shape
    assert d_model == D_MODEL and seq % S5_CHUNK == 0
    xt = x.reshape(batch * seq, D_MODEL)
    for l in range(DEPTH):
        w_in_bf = _permute_in_proj(w_in[l]).astype(BF16)
        proj = _matmul(xt, w_in_bf, gain=norm1_g[l], tm=1024, tn=512)
        bmat, cmat, pw = _s5_discretise(ssm_a_re[l], ssm_a_im[l], ssm_b_re[l], ssm_b_im[l],
                                        ssm_c_re[l], ssm_c_im[l], ssm_log_dt[l])
        y_ssm = _s5_scan(proj, ssm_d[l], bmat, cmat, pw, batch=batch, seq=seq)
        y_a = _s5_glu(y_ssm, w_glu[l].astype(BF16), b_glu[l], tm=512)
        y_b = _swa(proj, q_norm_g[l], k_norm_g[l], attn_sinks[l], batch=batch, seq=seq)
        y_c = _sgu(proj, sgu_ln_g[l], sgu_ln_b[l], sgu_w[l], sgu_b[l], tm=256)
        merged = _merge(y_a, y_b, y_c, w_branch[l].astype(BF16), proj, tm=512, tn=512)
        xt = _matmul(merged, w_out[l].astype(BF16), residual=xt, tm=1024, tn=512)
        q = _matmul(xt, w_query[l].astype(BF16), gain=norm2_g[l], tm=1024, tn=512)
        s1, w1, s2, w2, thr = _peer_select(q, peer_keys[l].astype(BF16), tm=256)
        xt = _peer_ffn(xt, norm2_g[l], peer_u[l].astype(BF16), peer_v[l].T.astype(BF16),
                       s1, w1, s2, w2, thr, tm=512, te=1024)
    return xt.reshape(batch, seq, D_MODEL)
```

```python
import functools
import math

import jax
import jax.numpy as jnp
from jax import lax
from jax.experimental import pallas as pl
from jax.experimental.pallas import tpu as pltpu

F32 = jnp.float32
BF16 = jnp.bfloat16

D_MODEL = 2048
DEPTH = 2
SSM_WIDTH = 1024
SSM_GROUP = 16
SSM_GROUPS = SSM_WIDTH // SSM_GROUP
SSM_STATE = 64
HEAD_DIM = 64
N_Q_HEADS = 16
N_KV_HEADS = 4
Q_PER_KV = N_Q_HEADS // N_KV_HEADS
WINDOW = 128
ATTN_WIDTH = N_Q_HEADS * HEAD_DIM
KV_WIDTH = N_KV_HEADS * HEAD_DIM
SGU_WIDTH = 1024
SGU_HEADS = 8
SGU_HEAD_DIM = SGU_WIDTH // SGU_HEADS
SGU_CHUNK = 128
N_BRANCH = 3
BRANCH_WIDTH = 1024
PEER_HEADS = 8
PEER_QDIM = 256
PEER_HALF = PEER_QDIM // 2
N_KEYS = 128
N_EXPERTS = N_KEYS * N_KEYS
PEER_TOPK = 16
EPS = 1e-6

V7X_VMEM_BYTES = 64 * 1024 * 1024
SUBLANES = 8
LANES = 128
MXU_TILE = 256

COL_SSM = 0
COL_Q = COL_SSM + SSM_WIDTH
COL_SGU = COL_Q + ATTN_WIDTH
COL_GATE = COL_SGU + 2 * SGU_WIDTH
COL_K = COL_GATE + N_BRANCH * D_MODEL
COL_V = COL_K + KV_WIDTH
IN_COLS = COL_V + KV_WIDTH

S5_GB = 16
S5_NGB = SSM_GROUPS // S5_GB
S5_IN = S5_GB * SSM_GROUP
S5_ST = S5_GB * SSM_STATE
S5_CHUNK = 256

SQRT_HALF = math.sqrt(0.5)


def _vmem_limit(pipelined_bytes, resident_bytes):
    need = 2 * pipelined_bytes + resident_bytes + (4 << 20)
    return int(min(need, V7X_VMEM_BYTES - (8 << 20)))


def _gelu(x):
    return 0.5 * x * (1.0 + lax.erf(x * SQRT_HALF))


def _rms(x, g):
    return x * lax.rsqrt(jnp.mean(x * x, axis=-1, keepdims=True) + EPS) * g


def _matmul_kernel(*refs, norm, residual):
    refs = list(refs)
    a_ref = refs.pop(0)
    g_ref = refs.pop(0) if norm else None
    w_ref = refs.pop(0)
    r_ref = refs.pop(0) if residual else None
    o_ref, abf_ref = refs

    @pl.when(pl.program_id(1) == 0)
    def _prepare_lhs():
        a = a_ref[...]
        if norm:
            a = _rms(a, g_ref[...])
        abf_ref[...] = a.astype(BF16)

    acc = jnp.dot(abf_ref[...], w_ref[...], preferred_element_type=F32)
    if residual:
        acc = r_ref[...] + acc
    o_ref[...] = acc


def _matmul(a, w, *, gain=None, residual=None, tm, tn):
    m, k = a.shape
    n = w.shape[1]
    assert m % tm == 0 and n % tn == 0
    operands = [a]
    in_specs = [pl.BlockSpec((tm, k), lambda i, j: (i, 0))]
    if gain is not None:
        operands.append(gain.reshape(1, k))
        in_specs.append(pl.BlockSpec((1, k), lambda i, j: (0, 0)))
    operands.append(w)
    in_specs.append(pl.BlockSpec((k, tn), lambda i, j: (0, j)))
    if residual is not None:
        operands.append(residual)
        in_specs.append(pl.BlockSpec((tm, tn), lambda i, j: (i, j)))
    pipelined = tm * k * 4 + k * tn * 2 + tm * tn * 4 * (2 if residual is not None else 1)
    return pl.pallas_call(
        functools.partial(_matmul_kernel, norm=gain is not None, residual=residual is not None),
        grid=(m // tm, n // tn),
        in_specs=in_specs,
        out_specs=pl.BlockSpec((tm, tn), lambda i, j: (i, j)),
        out_shape=jax.ShapeDtypeStruct((m, n), F32),
        scratch_shapes=[pltpu.VMEM((tm, k), BF16)],
        compiler_params=pltpu.CompilerParams(
            dimension_semantics=("parallel", "arbitrary"),
            vmem_limit_bytes=_vmem_limit(pipelined, tm * k * 2 + tm * k * 4)),
        name="matmul",
    )(*operands)


def _s5_kernel(u_ref, d_ref, bm_ref, cm_ref, pw_ref, o_ref, x_ref, carry_ref):
    tc = u_ref.shape[0]
    ng = tc // SUBLANES

    @pl.when(pl.program_id(2) == 0)
    def _reset_state():
        carry_ref[...] = jnp.zeros_like(carry_ref)

    u = u_ref[...]
    bu = jnp.dot(u.astype(BF16), bm_ref[0], preferred_element_type=F32)
    re = bu[:, :S5_ST].reshape(ng, SUBLANES, S5_ST)
    im = bu[:, S5_ST:].reshape(ng, SUBLANES, S5_ST)
    for idx, k in enumerate((1, 2, 4)):
        lre = pw_ref[0, 2 * idx]
        lim = pw_ref[0, 2 * idx + 1]
        sre = pltpu.roll(re, k, 1)
        sim = pltpu.roll(im, k, 1)
        re, im = re + (lre * sre - lim * sim), im + (lre * sim + lim * sre)
    x_ref[:, :S5_ST] = re.reshape(tc, S5_ST)
    x_ref[:, S5_ST:] = im.reshape(tc, S5_ST)

    cyre = pw_ref[0, 6]
    cyim = pw_ref[0, 7]

    def _carry_group(g, carry):
        cr, ci = carry
        r0 = pl.multiple_of(g * SUBLANES, SUBLANES)
        xr = x_ref[pl.ds(r0, SUBLANES), :S5_ST] + (cyre * cr - cyim * ci)
        xi = x_ref[pl.ds(r0, SUBLANES), S5_ST:] + (cyre * ci + cyim * cr)
        x_ref[pl.ds(r0, SUBLANES), :S5_ST] = xr
        x_ref[pl.ds(r0, SUBLANES), S5_ST:] = xi
        last = SUBLANES - 1
        return (jnp.broadcast_to(xr[last:, :], (SUBLANES, S5_ST)),
                jnp.broadcast_to(xi[last:, :], (SUBLANES, S5_ST)))

    cr, ci = lax.fori_loop(0, ng, _carry_group, (carry_ref[0], carry_ref[1]))
    carry_ref[0] = cr
    carry_ref[1] = ci

    y = jnp.dot(x_ref[...].astype(BF16), cm_ref[0], preferred_element_type=F32)
    o_ref[...] = y + d_ref[...] * u


def _s5_discretise(a_re, a_im, b_re, b_im, c_re, c_im, log_dt):
    dt = jnp.exp(log_dt)[:, None]

    def a_pow(k):
        mag = jnp.exp(a_re * dt * k)
        return mag * jnp.cos(a_im * dt * k), mag * jnp.sin(a_im * dt * k)

    abar_re, abar_im = a_pow(1.0)
    num_re, num_im = abar_re - 1.0, abar_im
    den = a_re * a_re + a_im * a_im
    coef_re = (num_re * a_re + num_im * a_im) / den
    coef_im = (num_im * a_re - num_re * a_im) / den
    bbar_re = coef_re[..., None] * b_re - coef_im[..., None] * b_im
    bbar_im = coef_re[..., None] * b_im + coef_im[..., None] * b_re

    eye = jnp.eye(S5_GB, dtype=F32)

    def in_block(b):
        b = b.reshape(S5_NGB, S5_GB, SSM_STATE, SSM_GROUP)
        return jnp.einsum('bgph,gk->bghkp', b, eye).reshape(S5_NGB, S5_IN, S5_ST)

    def out_block(c):
        c = c.reshape(S5_NGB, S5_GB, SSM_GROUP, SSM_STATE)
        return jnp.einsum('bghp,gk->bgpkh', c, eye).reshape(S5_NGB, S5_ST, S5_IN)

    bmat = jnp.concatenate([in_block(bbar_re), in_block(bbar_im)], axis=2).astype(BF16)
    cmat = jnp.concatenate([out_block(c_re), out_block(-c_im)], axis=1).astype(BF16)

    rows = jnp.arange(SUBLANES)[:, None, None]

    def tile(vals):
        return vals.reshape(SUBLANES, S5_NGB, S5_ST).transpose(1, 0, 2)

    tiles = []
    for k in (1, 2, 4):
        pr, pi = a_pow(float(k))
        tiles.append(tile(jnp.where(rows >= k, pr[None], 0.0)))
        tiles.append(tile(jnp.where(rows >= k, pi[None], 0.0)))
    steps = (rows + 1).astype(F32)
    mag = jnp.exp(a_re[None] * dt[None] * steps)
    tiles.append(tile(mag * jnp.cos(a_im[None] * dt[None] * steps)))
    tiles.append(tile(mag * jnp.sin(a_im[None] * dt[None] * steps)))
    pw = jnp.stack(tiles, axis=1)
    return bmat, cmat, pw


def _s5_scan(proj, ssm_d, bmat, cmat, pw, *, batch, seq):
    n = batch * seq
    nc = seq // S5_CHUNK
    pipelined = S5_CHUNK * S5_IN * 4 * 2 + S5_IN * 2 * S5_ST * 2 * 2 + 8 * SUBLANES * S5_ST * 4
    resident = S5_CHUNK * 2 * S5_ST * 4 * 6
    return pl.pallas_call(
        _s5_kernel,
        grid=(S5_NGB, batch, nc),
        in_specs=[
            pl.BlockSpec((S5_CHUNK, S5_IN), lambda g, b, c: (b * nc + c, COL_SSM // S5_IN + g)),
            pl.BlockSpec((1, S5_IN), lambda g, b, c: (0, g)),
            pl.BlockSpec((1, S5_IN, 2 * S5_ST), lambda g, b, c: (g, 0, 0)),
            pl.BlockSpec((1, 2 * S5_ST, S5_IN), lambda g, b, c: (g, 0, 0)),
            pl.BlockSpec((1, 8, SUBLANES, S5_ST), lambda g, b, c: (g, 0, 0, 0)),
        ],
        out_specs=pl.BlockSpec((S5_CHUNK, S5_IN), lambda g, b, c: (b * nc + c, g)),
        out_shape=jax.ShapeDtypeStruct((n, SSM_WIDTH), F32),
        scratch_shapes=[pltpu.VMEM((S5_CHUNK, 2 * S5_ST), F32),
                        pltpu.VMEM((2, SUBLANES, S5_ST), F32)],
        compiler_params=pltpu.CompilerParams(
            dimension_semantics=("parallel", "parallel", "arbitrary"),
            vmem_limit_bytes=_vmem_limit(pipelined, resident)),
        name="s5_scan",
    )(proj, ssm_d.reshape(1, SSM_WIDTH), bmat, cmat, pw)


def _s5_glu_kernel(y_ref, w_ref, b_ref, o_ref):
    y = _gelu(y_ref[...])
    z = jnp.dot(y.astype(BF16), w_ref[...], preferred_element_type=F32) + b_ref[...]
    o_ref[...] = y * jax.nn.sigmoid(z)


def _s5_glu(y, w_glu, b_glu, *, tm):
    n = y.shape[0]
    return pl.pallas_call(
        _s5_glu_kernel,
        grid=(n // tm,),
        in_specs=[pl.BlockSpec((tm, SSM_WIDTH), lambda i: (i, 0)),
                  pl.BlockSpec((SSM_WIDTH, SSM_WIDTH), lambda i: (0, 0)),
                  pl.BlockSpec((1, SSM_WIDTH), lambda i: (0, 0))],
        out_specs=pl.BlockSpec((tm, SSM_WIDTH), lambda i: (i, 0)),
        out_shape=jax.ShapeDtypeStruct((n, SSM_WIDTH), F32),
        compiler_params=pltpu.CompilerParams(
            dimension_semantics=("parallel",),
            vmem_limit_bytes=_vmem_limit(2 * tm * SSM_WIDTH * 4 + SSM_WIDTH * SSM_WIDTH * 2,
                                         4 * tm * SSM_WIDTH * 4)),
        name="s5_glu",
    )(y, w_glu, b_glu.reshape(1, SSM_WIDTH))


def _swa_kernel(sink_ref, q_ref, kp_ref, kc_ref, vp_ref, vc_ref, qg_ref, kg_ref, o_ref, *, blocks_per_seq):
    first = (pl.program_id(0) % blocks_per_seq) == 0
    q = q_ref[...]
    k = jnp.concatenate([kp_ref[...], kc_ref[...]], axis=0)
    v = jnp.concatenate([vp_ref[...], vc_ref[...]], axis=0)
    qpos = lax.broadcasted_iota(jnp.int32, (WINDOW, 2 * WINDOW), 0) + WINDOW
    kpos = lax.broadcasted_iota(jnp.int32, (WINDOW, 2 * WINDOW), 1)
    diff = qpos - kpos
    first_key = jnp.where(first, WINDOW, 0)
    valid = (diff >= 0) & (diff < WINDOW) & (kpos >= first_key)
    outs = []
    for kv in range(N_KV_HEADS):
        ks = slice(kv * HEAD_DIM, (kv + 1) * HEAD_DIM)
        k_h = _rms(k[:, ks], kg_ref[...]).astype(BF16)
        v_h = v[:, ks].astype(BF16)
        for g in range(Q_PER_KV):
            h = kv * Q_PER_KV + g
            q_h = _rms(q[:, h * HEAD_DIM:(h + 1) * HEAD_DIM], qg_ref[...]).astype(BF16)
            s = lax.dot_general(q_h, k_h, (((1,), (1,)), ((), ())), preferred_element_type=F32)
            s = jnp.where(valid, s * (HEAD_DIM ** -0.5), -1e30)
            sink = sink_ref[h]
            m = jnp.maximum(jnp.max(s, axis=-1, keepdims=True), sink)
            p = jnp.exp(s - m)
            denom = jnp.sum(p, axis=-1, keepdims=True) + jnp.exp(sink - m)
            o_h = jnp.dot(p.astype(BF16), v_h, preferred_element_type=F32)
            outs.append(o_h / denom)
    o_ref[...] = jnp.concatenate(outs, axis=-1)


def _swa(proj, q_g, k_g, sinks, *, batch, seq):
    n = batch * seq
    nb = seq // WINDOW
    prev = lambda r: jnp.maximum(r - 1, 0)
    pipelined = WINDOW * (2 * ATTN_WIDTH + 4 * KV_WIDTH) * 4
    return pl.pallas_call(
        functools.partial(_swa_kernel, blocks_per_seq=nb),
        grid=(n // WINDOW,),
        in_specs=[
            pl.BlockSpec(memory_space=pltpu.SMEM),
            pl.BlockSpec((WINDOW, ATTN_WIDTH), lambda r: (r, COL_Q // ATTN_WIDTH)),
            pl.BlockSpec((WINDOW, KV_WIDTH), lambda r: (prev(r), COL_K // KV_WIDTH)),
            pl.BlockSpec((WINDOW, KV_WIDTH), lambda r: (r, COL_K // KV_WIDTH)),
            pl.BlockSpec((WINDOW, KV_WIDTH), lambda r: (prev(r), COL_V // KV_WIDTH)),
            pl.BlockSpec((WINDOW, KV_WIDTH), lambda r: (r, COL_V // KV_WIDTH)),
            pl.BlockSpec((1, HEAD_DIM), lambda r: (0, 0)),
            pl.BlockSpec((1, HEAD_DIM), lambda r: (0, 0)),
        ],
        out_specs=pl.BlockSpec((WINDOW, ATTN_WIDTH), lambda r: (r, 0)),
        out_shape=jax.ShapeDtypeStruct((n, ATTN_WIDTH), F32),
        compiler_params=pltpu.CompilerParams(
            dimension_semantics=("parallel",),
            vmem_limit_bytes=_vmem_limit(pipelined, 16 << 20)),
        name="swa",
    )(sinks, proj, proj, proj, proj, proj, q_g.reshape(1, HEAD_DIM), k_g.reshape(1, HEAD_DIM))


def _sgu_kernel(z_ref, g_ref, b_ref, w_ref, bs_ref, o_ref):
    tm = z_ref.shape[0]
    z = _gelu(z_ref[...])
    u = z[:, :SGU_WIDTH]
    v = z[:, SGU_WIDTH:]
    vc = v - jnp.mean(v, axis=-1, keepdims=True)
    vn = vc * lax.rsqrt(jnp.mean(vc * vc, axis=-1, keepdims=True) + EPS) * g_ref[...] + b_ref[...]
    vn = vn.astype(BF16)
    row = lax.broadcasted_iota(jnp.int32, (SGU_CHUNK, SGU_CHUNK), 0)
    col = lax.broadcasted_iota(jnp.int32, (SGU_CHUNK, SGU_CHUNK), 1)
    for h in range(SGU_HEADS):
        w_h = jnp.where(row >= col, w_ref[h], 0.0).astype(BF16)
        cs = slice(h * SGU_HEAD_DIM, (h + 1) * SGU_HEAD_DIM)
        for c in range(tm // SGU_CHUNK):
            rs = slice(c * SGU_CHUNK, (c + 1) * SGU_CHUNK)
            mixed = jnp.dot(w_h, vn[rs, cs], preferred_element_type=F32) + bs_ref[:, h:h + 1]
            o_ref[rs, cs] = u[rs, cs] * mixed


def _sgu(proj, ln_g, ln_b, w_s, b_s, *, tm):
    n = proj.shape[0]
    return pl.pallas_call(
        _sgu_kernel,
        grid=(n // tm,),
        in_specs=[
            pl.BlockSpec((tm, 2 * SGU_WIDTH), lambda i: (i, COL_SGU // (2 * SGU_WIDTH))),
            pl.BlockSpec((1, SGU_WIDTH), lambda i: (0, 0)),
            pl.BlockSpec((1, SGU_WIDTH), lambda i: (0, 0)),
            pl.BlockSpec((SGU_HEADS, SGU_CHUNK, SGU_CHUNK), lambda i: (0, 0, 0)),
            pl.BlockSpec((SGU_CHUNK, SGU_HEADS), lambda i: (0, 0)),
        ],
        out_specs=pl.BlockSpec((tm, SGU_WIDTH), lambda i: (i, 0)),
        out_shape=jax.ShapeDtypeStruct((n, SGU_WIDTH), F32),
        compiler_params=pltpu.CompilerParams(
            dimension_semantics=("parallel",),
            vmem_limit_bytes=_vmem_limit(tm * 3 * SGU_WIDTH * 4, 6 * tm * 2 * SGU_WIDTH * 4)),
        name="sgu",
    )(proj, ln_g.reshape(1, SGU_WIDTH), ln_b.reshape(1, SGU_WIDTH), w_s, b_s.T)


def _merge_kernel(ya_ref, yb_ref, yc_ref, w_ref, ga_ref, gb_ref, gc_ref, o_ref, ybf_ref):
    @pl.when(pl.program_id(1) == 0)
    def _cast_branches():
        for n, y_ref in enumerate((ya_ref, yb_ref, yc_ref)):
            ybf_ref[n] = y_ref[...].astype(BF16)

    acc = None
    for n, gate_ref in enumerate((ga_ref, gb_ref, gc_ref)):
        branch = jnp.dot(ybf_ref[n], w_ref[n], preferred_element_type=F32)
        term = jax.nn.sigmoid(gate_ref[...]) * branch
        acc = term if acc is None else acc + term
    o_ref[...] = acc


def _merge(ya, yb, yc, w_branch, proj, *, tm, tn):
    n = ya.shape[0]
    y_spec = pl.BlockSpec((tm, BRANCH_WIDTH), lambda i, j: (i, 0))

    def gate_spec(b):
        return pl.BlockSpec((tm, tn), lambda i, j: (i, (COL_GATE + b * D_MODEL) // tn + j))

    pipelined = 3 * tm * BRANCH_WIDTH * 4 + 3 * BRANCH_WIDTH * tn * 2 + 4 * tm * tn * 4
    return pl.pallas_call(
        _merge_kernel,
        grid=(n // tm, D_MODEL // tn),
        in_specs=[y_spec, y_spec, y_spec,
                  pl.BlockSpec((N_BRANCH, BRANCH_WIDTH, tn), lambda i, j: (0, 0, j)),
                  gate_spec(0), gate_spec(1), gate_spec(2)],
        out_specs=pl.BlockSpec((tm, tn), lambda i, j: (i, j)),
        out_shape=jax.ShapeDtypeStruct((n, D_MODEL), F32),
        scratch_shapes=[pltpu.VMEM((N_BRANCH, tm, BRANCH_WIDTH), BF16)],
        compiler_params=pltpu.CompilerParams(
            dimension_semantics=("parallel", "arbitrary"),
            vmem_limit_bytes=_vmem_limit(pipelined, 3 * tm * BRANCH_WIDTH * 2 + 4 * tm * tn * 4)),
        name="merge",
    )(ya, yb, yc, w_branch, proj, proj, proj)


def _top_values(s, count):
    rows = []
    for _ in range(count):
        m = jnp.max(s, axis=0, keepdims=True)
        rows.append(m)
        s = jnp.where(s == m, -jnp.inf, s)
    return jnp.concatenate(rows, axis=0)


def _peer_select_kernel(q_ref, keys_ref, c_ref, w1_ref, s2_ref, w2_ref):
    nt = lax.dot_general
    contract_last = (((1,), (1,)), ((), ()))
    for h in range(PEER_HEADS):
        c0 = h * PEER_QDIM
        q1 = q_ref[:, c0:c0 + PEER_HALF].astype(BF16)
        q2 = q_ref[:, c0 + PEER_HALF:c0 + PEER_QDIM].astype(BF16)
        s1 = nt(keys_ref[0, h], q1, contract_last, preferred_element_type=F32)
        s2 = nt(keys_ref[1, h], q2, contract_last, preferred_element_type=F32)
        a = _top_values(s1, PEER_TOPK)
        b = _top_values(s2, PEER_TOPK)
        cand = [a[0:1] + b]
        cand += [a[p:p + 1] + b[0:SUBLANES] for p in range(1, SUBLANES)]
        cand.append(a[SUBLANES:] + b[0:1])
        cand = jnp.concatenate(cand, axis=0)
        work = cand
        for _ in range(PEER_TOPK - 1):
            m = jnp.max(work, axis=0, keepdims=True)
            work = jnp.where(work == m, -jnp.inf, work)
        thr = jnp.max(work, axis=0, keepdims=True)
        top = a[0:1] + b[0:1]
        z = jnp.sum(jnp.where(cand >= thr, jnp.exp(cand - top), 0.0), axis=0, keepdims=True)
        cut = jnp.full(s1.shape, jnp.inf, F32)
        for r in range(PEER_TOPK):
            b_r = b[r:r + 1]
            cut = jnp.minimum(cut, jnp.where(s1 + b_r >= thr, b_r, jnp.inf))
        c_ref[h] = cut
        s2_ref[h] = s2
        w1_ref[h] = jnp.exp(s1 - a[0:1])
        w2_ref[h] = jnp.exp(s2 - b[0:1]) / z


def _peer_select(q, keys, *, tm):
    n = q.shape[0]
    big = jax.ShapeDtypeStruct((PEER_HEADS, N_KEYS, n), F32)
    big_spec = pl.BlockSpec((PEER_HEADS, N_KEYS, tm), lambda i: (0, 0, i))
    return pl.pallas_call(
        _peer_select_kernel,
        grid=(n // tm,),
        in_specs=[pl.BlockSpec((tm, PEER_HEADS * PEER_QDIM), lambda i: (i, 0)),
                  pl.BlockSpec((2, PEER_HEADS, N_KEYS, PEER_HALF), lambda i: (0, 0, 0, 0))],
        out_specs=[big_spec, big_spec, big_spec, big_spec],
        out_shape=[big, big, big, big],
        compiler_params=pltpu.CompilerParams(
            dimension_semantics=("parallel",),
            vmem_limit_bytes=_vmem_limit(tm * PEER_HEADS * PEER_QDIM * 4 + 4 * PEER_HEADS * N_KEYS * tm * 4,
                                         16 << 20)),
        name="peer_select",
    )(q, keys)


def _peer_stage(u_ref, vt_ref, c_ref, w1_ref, s2_ref, w2_ref, xn_ref, acc_ref, h_new, h_old):
    te, tm = h_new.shape

    def gate_block(i, c):
        rs = slice(i * N_KEYS, (i + 1) * N_KEYS)
        ls = slice(c * LANES, (c + 1) * LANES)
        gate = None
        for hd in range(PEER_HEADS):
            hit = s2_ref[hd, :, ls] >= c_ref[hd, i, :, ls]
            term = jnp.where(hit, w2_ref[hd, :, ls], 0.0) * w1_ref[hd, i, :, ls]
            gate = term if gate is None else gate + term
        return (_gelu(h_old[rs, ls]) * gate).astype(BF16)

    def output_piece(t):
        cols = [jnp.concatenate([gate_block(i, c) for i in range(te // N_KEYS)], axis=0)
                for c in range(t // LANES, (t + MXU_TILE) // LANES)]
        a = jnp.concatenate(cols, axis=1)
        acc_ref[:, t:t + MXU_TILE] += jnp.dot(vt_ref[...], a, preferred_element_type=F32)

    def hidden_piece(r):
        rs = slice(r, r + MXU_TILE)
        h_new[rs, :] = lax.dot_general(u_ref[rs, :], xn_ref[...], (((1,), (1,)), ((), ())),
                                       preferred_element_type=F32)

    assert te // MXU_TILE == tm // MXU_TILE
    for k in range(tm // MXU_TILE):
        hidden_piece(k * MXU_TILE)
        output_piece(k * MXU_TILE)


def _peer_ffn_kernel(x_ref, g_ref, u_ref, vt_ref, c_ref, w1_ref, s2_ref, w2_ref, o_ref,
                     xn_ref, acc_ref, h0_ref, h1_ref):
    s = pl.program_id(1)

    @pl.when(s == 0)
    def _start_token_tile():
        xn_ref[...] = _rms(x_ref[...], g_ref[...]).astype(BF16)
        for ref in (acc_ref, h0_ref, h1_ref):
            ref[...] = jnp.zeros_like(ref)

    stage = functools.partial(_peer_stage, u_ref, vt_ref, c_ref, w1_ref, s2_ref, w2_ref, xn_ref, acc_ref)

    @pl.when(s % 2 == 0)
    def _even_step():
        stage(h0_ref, h1_ref)

    @pl.when(s % 2 == 1)
    def _odd_step():
        stage(h1_ref, h0_ref)

    @pl.when(s == pl.num_programs(1) - 1)
    def _finish_token_tile():
        o_ref[...] = x_ref[...] + acc_ref[...].T


PEER_PIPELINE_LAG = 1


def _peer_ffn(x, gain, u_bf, vt_bf, cut, w1, s2, w2, *, tm, te):
    n = x.shape[0]
    slabs = te // N_KEYS
    tiles = N_EXPERTS // te
    clip = lambda s, lag: jnp.clip(s - lag, 0, tiles - 1)
    row_spec = pl.BlockSpec((PEER_HEADS, slabs, 1, tm), lambda t, s: (0, clip(s, PEER_PIPELINE_LAG), 0, t))
    cut = cut.reshape(PEER_HEADS, N_KEYS, 1, n)
    w1 = w1.reshape(PEER_HEADS, N_KEYS, 1, n)
    full_spec = pl.BlockSpec((PEER_HEADS, N_KEYS, tm), lambda t, s: (0, 0, t))
    pipelined = (2 * tm * D_MODEL * 4 + 2 * te * D_MODEL * 2
                 + 2 * PEER_HEADS * (slabs * SUBLANES + N_KEYS) * tm * 4)
    resident = tm * D_MODEL * 2 + D_MODEL * tm * 4 + 2 * te * tm * 6 + D_MODEL * tm * 4
    return pl.pallas_call(
        _peer_ffn_kernel,
        grid=(n // tm, tiles + PEER_PIPELINE_LAG),
        in_specs=[
            pl.BlockSpec((tm, D_MODEL), lambda t, s: (t, 0)),
            pl.BlockSpec((1, D_MODEL), lambda t, s: (0, 0)),
            pl.BlockSpec((te, D_MODEL), lambda t, s: (clip(s, 0), 0)),
            pl.BlockSpec((D_MODEL, te), lambda t, s: (0, clip(s, PEER_PIPELINE_LAG))),
            row_spec, row_spec, full_spec, full_spec,
        ],
        out_specs=pl.BlockSpec((tm, D_MODEL), lambda t, s: (t, 0)),
        out_shape=jax.ShapeDtypeStruct((n, D_MODEL), F32),
        scratch_shapes=[pltpu.VMEM((tm, D_MODEL), BF16),
                        pltpu.VMEM((D_MODEL, tm), F32),
                        pltpu.VMEM((te, tm), F32), pltpu.VMEM((te, tm), F32)],
        compiler_params=pltpu.CompilerParams(
            dimension_semantics=("parallel", "arbitrary"),
            vmem_limit_bytes=_vmem_limit(pipelined, resident)),
        name="peer_ffn",
    )(x, gain.reshape(1, D_MODEL), u_bf, vt_bf, cut, w1, s2, w2)


def _permute_in_proj(w):
    ssm_q = SSM_WIDTH + ATTN_WIDTH
    kv_end = ssm_q + 2 * KV_WIDTH
    sgu_end = kv_end + 2 * SGU_WIDTH
    return jnp.concatenate([w[:, :ssm_q], w[:, kv_end:sgu_end], w[:, sgu_end:], w[:, ssm_q:kv_end]], axis=1)


def kernel(x, norm1_g, w_in, ssm_a_re, ssm_a_im, ssm_b_re, ssm_b_im, ssm_c_re, ssm_c_im, ssm_d, ssm_log_dt,
           w_glu, b_glu, q_norm_g, k_norm_g, attn_sinks, sgu_ln_g, sgu_ln_b, sgu_w, sgu_b, w_branch, w_out,
           norm2_g, w_query, peer_keys, peer_u, peer_v):
    batch, seq, d_model = x.shape
    assert d_model == D_MODEL and seq % S5_CHUNK == 0
    xt = x.reshape(batch * seq, D_MODEL)
    for l in range(DEPTH):
        w_in_bf = _permute_in_proj(w_in[l]).astype(BF16)
        proj = _matmul(xt, w_in_bf, gain=norm1_g[l], tm=1024, tn=512)
        bmat, cmat, pw = _s5_discretise(ssm_a_re[l], ssm_a_im[l], ssm_b_re[l], ssm_b_im[l],
                                        ssm_c_re[l], ssm_c_im[l], ssm_log_dt[l])
        y_ssm = _s5_scan(proj, ssm_d[l], bmat, cmat, pw, batch=batch, seq=seq)
        y_a = _s5_glu(y_ssm, w_glu[l].astype(BF16), b_glu[l], tm=512)
        y_b = _swa(proj, q_norm_g[l], k_norm_g[l], attn_sinks[l], batch=batch, seq=seq)
        y_c = _sgu(proj, sgu_ln_g[l], sgu_ln_b[l], sgu_w[l], sgu_b[l], tm=256)
        merged = _merge(y_a, y_b, y_c, w_branch[l].astype(BF16), proj, tm=512, tn=512)
        xt = _matmul(merged, w_out[l].astype(BF16), residual=xt, tm=1024, tn=512)
        q = _matmul(xt, w_query[l].astype(BF16), gain=norm2_g[l], tm=1024, tn=512)
        cut, w1, s2, w2 = _peer_select(q, peer_keys[l].astype(BF16), tm=256)
        xt = _peer_ffn(xt, norm2_g[l], peer_u[l].astype(BF16), peer_v[l].T.astype(BF16),
                       cut, w1, s2, w2, tm=512, te=512)
    return xt.reshape(batch, seq, D_MODEL)
```

```python
import functools
import math

import jax
import jax.numpy as jnp
from jax import lax
from jax.experimental import pallas as pl
from jax.experimental.pallas import tpu as pltpu

F32 = jnp.float32
BF16 = jnp.bfloat16

D_MODEL = 2048
DEPTH = 2
SSM_WIDTH = 1024
SSM_GROUP = 16
SSM_GROUPS = SSM_WIDTH // SSM_GROUP
SSM_STATE = 64
HEAD_DIM = 64
N_Q_HEADS = 16
N_KV_HEADS = 4
Q_PER_KV = N_Q_HEADS // N_KV_HEADS
WINDOW = 128
ATTN_WIDTH = N_Q_HEADS * HEAD_DIM
KV_WIDTH = N_KV_HEADS * HEAD_DIM
SGU_WIDTH = 1024
SGU_HEADS = 8
SGU_HEAD_DIM = SGU_WIDTH // SGU_HEADS
SGU_CHUNK = 128
N_BRANCH = 3
BRANCH_WIDTH = 1024
PEER_HEADS = 8
PEER_QDIM = 256
PEER_HALF = PEER_QDIM // 2
N_KEYS = 128
N_EXPERTS = N_KEYS * N_KEYS
PEER_TOPK = 16
EPS = 1e-6

V7X_VMEM_BYTES = 64 * 1024 * 1024
SUBLANES = 8
LANES = 128
MXU_TILE = 256

COL_SSM = 0
COL_Q = COL_SSM + SSM_WIDTH
COL_SGU = COL_Q + ATTN_WIDTH
COL_GATE = COL_SGU + 2 * SGU_WIDTH
COL_K = COL_GATE + N_BRANCH * D_MODEL
COL_V = COL_K + KV_WIDTH
IN_COLS = COL_V + KV_WIDTH

S5_GB = 16
S5_NGB = SSM_GROUPS // S5_GB
S5_IN = S5_GB * SSM_GROUP
S5_ST = S5_GB * SSM_STATE
S5_CHUNK = 256

SQRT_HALF = math.sqrt(0.5)


def _vmem_limit(pipelined_bytes, resident_bytes):
    need = 2 * pipelined_bytes + resident_bytes + (4 << 20)
    return int(min(need, V7X_VMEM_BYTES - (8 << 20)))


def _gelu(x):
    return 0.5 * x * (1.0 + lax.erf(x * SQRT_HALF))


def _rms(x, g):
    return x * lax.rsqrt(jnp.mean(x * x, axis=-1, keepdims=True) + EPS) * g


def _matmul_kernel(*refs, norm, residual):
    refs = list(refs)
    a_ref = refs.pop(0)
    g_ref = refs.pop(0) if norm else None
    w_ref = refs.pop(0)
    r_ref = refs.pop(0) if residual else None
    o_ref, abf_ref = refs

    @pl.when(pl.program_id(1) == 0)
    def _prepare_lhs():
        a = a_ref[...]
        if norm:
            a = _rms(a, g_ref[...])
        abf_ref[...] = a.astype(BF16)

    acc = jnp.dot(abf_ref[...], w_ref[...], preferred_element_type=F32)
    if residual:
        acc = r_ref[...] + acc
    o_ref[...] = acc


def _matmul(a, w, layer, *, gain=None, residual=None, out_block=lambda j: j, tm, tn):
    m, k = a.shape
    n = w.shape[2]
    assert m % tm == 0 and n % tn == 0
    operands = [a]
    in_specs = [pl.BlockSpec((tm, k), lambda i, j: (i, 0))]
    if gain is not None:
        operands.append(gain.reshape(1, k))
        in_specs.append(pl.BlockSpec((1, k), lambda i, j: (0, 0)))
    operands.append(w)
    in_specs.append(pl.BlockSpec((None, k, tn), lambda i, j: (layer, 0, j)))
    if residual is not None:
        operands.append(residual)
        in_specs.append(pl.BlockSpec((tm, tn), lambda i, j: (i, j)))
    pipelined = tm * k * 4 + k * tn * 2 + tm * tn * 4 * (2 if residual is not None else 1)
    return pl.pallas_call(
        functools.partial(_matmul_kernel, norm=gain is not None, residual=residual is not None),
        grid=(m // tm, n // tn),
        in_specs=in_specs,
        out_specs=pl.BlockSpec((tm, tn), lambda i, j: (i, out_block(j))),
        out_shape=jax.ShapeDtypeStruct((m, n), F32),
        scratch_shapes=[pltpu.VMEM((tm, k), BF16)],
        compiler_params=pltpu.CompilerParams(
            dimension_semantics=("parallel", "arbitrary"),
            vmem_limit_bytes=_vmem_limit(pipelined, tm * k * 2 + tm * k * 4)),
        name="matmul",
    )(*operands)


def _s5_kernel(u_ref, d_ref, bm_ref, cm_ref, pw_ref, o_ref, x_ref, carry_ref):
    tc = u_ref.shape[0]
    ng = tc // SUBLANES

    @pl.when(pl.program_id(2) == 0)
    def _reset_state():
        carry_ref[...] = jnp.zeros_like(carry_ref)

    u = u_ref[...]
    bu = jnp.dot(u.astype(BF16), bm_ref[0], preferred_element_type=F32)
    re = bu[:, :S5_ST].reshape(ng, SUBLANES, S5_ST)
    im = bu[:, S5_ST:].reshape(ng, SUBLANES, S5_ST)
    for idx, k in enumerate((1, 2, 4)):
        lre = pw_ref[0, 2 * idx]
        lim = pw_ref[0, 2 * idx + 1]
        sre = pltpu.roll(re, k, 1)
        sim = pltpu.roll(im, k, 1)
        re, im = re + (lre * sre - lim * sim), im + (lre * sim + lim * sre)
    x_ref[:, :S5_ST] = re.reshape(tc, S5_ST)
    x_ref[:, S5_ST:] = im.reshape(tc, S5_ST)

    cyre = pw_ref[0, 6]
    cyim = pw_ref[0, 7]

    def _carry_group(g, carry):
        cr, ci = carry
        r0 = pl.multiple_of(g * SUBLANES, SUBLANES)
        xr = x_ref[pl.ds(r0, SUBLANES), :S5_ST] + (cyre * cr - cyim * ci)
        xi = x_ref[pl.ds(r0, SUBLANES), S5_ST:] + (cyre * ci + cyim * cr)
        x_ref[pl.ds(r0, SUBLANES), :S5_ST] = xr
        x_ref[pl.ds(r0, SUBLANES), S5_ST:] = xi
        last = SUBLANES - 1
        return (jnp.broadcast_to(xr[last:, :], (SUBLANES, S5_ST)),
                jnp.broadcast_to(xi[last:, :], (SUBLANES, S5_ST)))

    cr, ci = lax.fori_loop(0, ng, _carry_group, (carry_ref[0], carry_ref[1]))
    carry_ref[0] = cr
    carry_ref[1] = ci

    y = jnp.dot(x_ref[...].astype(BF16), cm_ref[0], preferred_element_type=F32)
    o_ref[...] = y + d_ref[...] * u


def _s5_discretise(a_re, a_im, b_re, b_im, c_re, c_im, log_dt):
    dt = jnp.exp(log_dt)[:, None]

    def a_pow(k):
        mag = jnp.exp(a_re * dt * k)
        return mag * jnp.cos(a_im * dt * k), mag * jnp.sin(a_im * dt * k)

    abar_re, abar_im = a_pow(1.0)
    num_re, num_im = abar_re - 1.0, abar_im
    den = a_re * a_re + a_im * a_im
    coef_re = (num_re * a_re + num_im * a_im) / den
    coef_im = (num_im * a_re - num_re * a_im) / den
    bbar_re = coef_re[..., None] * b_re - coef_im[..., None] * b_im
    bbar_im = coef_re[..., None] * b_im + coef_im[..., None] * b_re

    eye = jnp.eye(S5_GB, dtype=F32)

    def in_block(b):
        b = b.reshape(S5_NGB, S5_GB, SSM_STATE, SSM_GROUP)
        return jnp.einsum('bgph,gk->bghkp', b, eye).reshape(S5_NGB, S5_IN, S5_ST)

    def out_block(c):
        c = c.reshape(S5_NGB, S5_GB, SSM_GROUP, SSM_STATE)
        return jnp.einsum('bghp,gk->bgpkh', c, eye).reshape(S5_NGB, S5_ST, S5_IN)

    bmat = jnp.concatenate([in_block(bbar_re), in_block(bbar_im)], axis=2).astype(BF16)
    cmat = jnp.concatenate([out_block(c_re), out_block(-c_im)], axis=1).astype(BF16)

    rows = jnp.arange(SUBLANES)[:, None, None]

    def tile(vals):
        return vals.reshape(SUBLANES, S5_NGB, S5_ST).transpose(1, 0, 2)

    tiles = []
    for k in (1, 2, 4):
        pr, pi = a_pow(float(k))
        tiles.append(tile(jnp.where(rows >= k, pr[None], 0.0)))
        tiles.append(tile(jnp.where(rows >= k, pi[None], 0.0)))
    steps = (rows + 1).astype(F32)
    mag = jnp.exp(a_re[None] * dt[None] * steps)
    tiles.append(tile(mag * jnp.cos(a_im[None] * dt[None] * steps)))
    tiles.append(tile(mag * jnp.sin(a_im[None] * dt[None] * steps)))
    pw = jnp.stack(tiles, axis=1)
    return bmat, cmat, pw


def _s5_scan(proj, ssm_d, bmat, cmat, pw, *, batch, seq):
    n = batch * seq
    nc = seq // S5_CHUNK
    pipelined = S5_CHUNK * S5_IN * 4 * 2 + S5_IN * 2 * S5_ST * 2 * 2 + 8 * SUBLANES * S5_ST * 4
    resident = S5_CHUNK * 2 * S5_ST * 4 * 6
    return pl.pallas_call(
        _s5_kernel,
        grid=(S5_NGB, batch, nc),
        in_specs=[
            pl.BlockSpec((S5_CHUNK, S5_IN), lambda g, b, c: (b * nc + c, COL_SSM // S5_IN + g)),
            pl.BlockSpec((1, S5_IN), lambda g, b, c: (0, g)),
            pl.BlockSpec((1, S5_IN, 2 * S5_ST), lambda g, b, c: (g, 0, 0)),
            pl.BlockSpec((1, 2 * S5_ST, S5_IN), lambda g, b, c: (g, 0, 0)),
            pl.BlockSpec((1, 8, SUBLANES, S5_ST), lambda g, b, c: (g, 0, 0, 0)),
        ],
        out_specs=pl.BlockSpec((S5_CHUNK, S5_IN), lambda g, b, c: (b * nc + c, g)),
        out_shape=jax.ShapeDtypeStruct((n, SSM_WIDTH), F32),
        scratch_shapes=[pltpu.VMEM((S5_CHUNK, 2 * S5_ST), F32),
                        pltpu.VMEM((2, SUBLANES, S5_ST), F32)],
        compiler_params=pltpu.CompilerParams(
            dimension_semantics=("parallel", "parallel", "arbitrary"),
            vmem_limit_bytes=_vmem_limit(pipelined, resident)),
        name="s5_scan",
    )(proj, ssm_d.reshape(1, SSM_WIDTH), bmat, cmat, pw)


def _s5_glu_kernel(y_ref, w_ref, b_ref, o_ref):
    y = _gelu(y_ref[...])
    z = jnp.dot(y.astype(BF16), w_ref[...], preferred_element_type=F32) + b_ref[...]
    o_ref[...] = y * jax.nn.sigmoid(z)


def _s5_glu(y, w_glu, layer, b_glu, *, tm):
    n = y.shape[0]
    return pl.pallas_call(
        _s5_glu_kernel,
        grid=(n // tm,),
        in_specs=[pl.BlockSpec((tm, SSM_WIDTH), lambda i: (i, 0)),
                  pl.BlockSpec((None, SSM_WIDTH, SSM_WIDTH), lambda i: (layer, 0, 0)),
                  pl.BlockSpec((1, SSM_WIDTH), lambda i: (0, 0))],
        out_specs=pl.BlockSpec((tm, SSM_WIDTH), lambda i: (i, 0)),
        out_shape=jax.ShapeDtypeStruct((n, SSM_WIDTH), F32),
        compiler_params=pltpu.CompilerParams(
            dimension_semantics=("parallel",),
            vmem_limit_bytes=_vmem_limit(2 * tm * SSM_WIDTH * 4 + SSM_WIDTH * SSM_WIDTH * 2,
                                         4 * tm * SSM_WIDTH * 4)),
        name="s5_glu",
    )(y, w_glu, b_glu.reshape(1, SSM_WIDTH))


def _swa_kernel(sink_ref, q_ref, kp_ref, kc_ref, vp_ref, vc_ref, qg_ref, kg_ref, o_ref, *, blocks_per_seq):
    first = (pl.program_id(0) % blocks_per_seq) == 0
    q = q_ref[...]
    k = jnp.concatenate([kp_ref[...], kc_ref[...]], axis=0)
    v = jnp.concatenate([vp_ref[...], vc_ref[...]], axis=0)
    qpos = lax.broadcasted_iota(jnp.int32, (WINDOW, 2 * WINDOW), 0) + WINDOW
    kpos = lax.broadcasted_iota(jnp.int32, (WINDOW, 2 * WINDOW), 1)
    diff = qpos - kpos
    first_key = jnp.where(first, WINDOW, 0)
    valid = (diff >= 0) & (diff < WINDOW) & (kpos >= first_key)
    outs = []
    for kv in range(N_KV_HEADS):
        ks = slice(kv * HEAD_DIM, (kv + 1) * HEAD_DIM)
        k_h = _rms(k[:, ks], kg_ref[...]).astype(BF16)
        v_h = v[:, ks].astype(BF16)
        for g in range(Q_PER_KV):
            h = kv * Q_PER_KV + g
            q_h = _rms(q[:, h * HEAD_DIM:(h + 1) * HEAD_DIM], qg_ref[...]).astype(BF16)
            s = lax.dot_general(q_h, k_h, (((1,), (1,)), ((), ())), preferred_element_type=F32)
            s = jnp.where(valid, s * (HEAD_DIM ** -0.5), -1e30)
            sink = sink_ref[h]
            m = jnp.maximum(jnp.max(s, axis=-1, keepdims=True), sink)
            p = jnp.exp(s - m)
            denom = jnp.sum(p, axis=-1, keepdims=True) + jnp.exp(sink - m)
            o_h = jnp.dot(p.astype(BF16), v_h, preferred_element_type=F32)
            outs.append(o_h / denom)
    o_ref[...] = jnp.concatenate(outs, axis=-1)


def _swa(proj, q_g, k_g, sinks, *, batch, seq):
    n = batch * seq
    nb = seq // WINDOW
    prev = lambda r: jnp.maximum(r - 1, 0)
    pipelined = WINDOW * (2 * ATTN_WIDTH + 4 * KV_WIDTH) * 4
    return pl.pallas_call(
        functools.partial(_swa_kernel, blocks_per_seq=nb),
        grid=(n // WINDOW,),
        in_specs=[
            pl.BlockSpec(memory_space=pltpu.SMEM),
            pl.BlockSpec((WINDOW, ATTN_WIDTH), lambda r: (r, COL_Q // ATTN_WIDTH)),
            pl.BlockSpec((WINDOW, KV_WIDTH), lambda r: (prev(r), COL_K // KV_WIDTH)),
            pl.BlockSpec((WINDOW, KV_WIDTH), lambda r: (r, COL_K // KV_WIDTH)),
            pl.BlockSpec((WINDOW, KV_WIDTH), lambda r: (prev(r), COL_V // KV_WIDTH)),
            pl.BlockSpec((WINDOW, KV_WIDTH), lambda r: (r, COL_V // KV_WIDTH)),
            pl.BlockSpec((1, HEAD_DIM), lambda r: (0, 0)),
            pl.BlockSpec((1, HEAD_DIM), lambda r: (0, 0)),
        ],
        out_specs=pl.BlockSpec((WINDOW, ATTN_WIDTH), lambda r: (r, 0)),
        out_shape=jax.ShapeDtypeStruct((n, ATTN_WIDTH), F32),
        compiler_params=pltpu.CompilerParams(
            dimension_semantics=("parallel",),
            vmem_limit_bytes=_vmem_limit(pipelined, 16 << 20)),
        name="swa",
    )(sinks, proj, proj, proj, proj, proj, q_g.reshape(1, HEAD_DIM), k_g.reshape(1, HEAD_DIM))


def _sgu_kernel(z_ref, g_ref, b_ref, w_ref, bs_ref, o_ref):
    tm = z_ref.shape[0]
    z = _gelu(z_ref[...])
    u = z[:, :SGU_WIDTH]
    v = z[:, SGU_WIDTH:]
    vc = v - jnp.mean(v, axis=-1, keepdims=True)
    vn = vc * lax.rsqrt(jnp.mean(vc * vc, axis=-1, keepdims=True) + EPS) * g_ref[...] + b_ref[...]
    vn = vn.astype(BF16)
    row = lax.broadcasted_iota(jnp.int32, (SGU_CHUNK, SGU_CHUNK), 0)
    col = lax.broadcasted_iota(jnp.int32, (SGU_CHUNK, SGU_CHUNK), 1)
    for h in range(SGU_HEADS):
        w_h = jnp.where(row >= col, w_ref[h], 0.0).astype(BF16)
        cs = slice(h * SGU_HEAD_DIM, (h + 1) * SGU_HEAD_DIM)
        for c in range(tm // SGU_CHUNK):
            rs = slice(c * SGU_CHUNK, (c + 1) * SGU_CHUNK)
            mixed = jnp.dot(w_h, vn[rs, cs], preferred_element_type=F32) + bs_ref[:, h:h + 1]
            o_ref[rs, cs] = u[rs, cs] * mixed


def _sgu(proj, ln_g, ln_b, w_s, b_s, *, tm):
    n = proj.shape[0]
    return pl.pallas_call(
        _sgu_kernel,
        grid=(n // tm,),
        in_specs=[
            pl.BlockSpec((tm, 2 * SGU_WIDTH), lambda i: (i, COL_SGU // (2 * SGU_WIDTH))),
            pl.BlockSpec((1, SGU_WIDTH), lambda i: (0, 0)),
            pl.BlockSpec((1, SGU_WIDTH), lambda i: (0, 0)),
            pl.BlockSpec((SGU_HEADS, SGU_CHUNK, SGU_CHUNK), lambda i: (0, 0, 0)),
            pl.BlockSpec((SGU_CHUNK, SGU_HEADS), lambda i: (0, 0)),
        ],
        out_specs=pl.BlockSpec((tm, SGU_WIDTH), lambda i: (i, 0)),
        out_shape=jax.ShapeDtypeStruct((n, SGU_WIDTH), F32),
        compiler_params=pltpu.CompilerParams(
            dimension_semantics=("parallel",),
            vmem_limit_bytes=_vmem_limit(tm * 3 * SGU_WIDTH * 4, 6 * tm * 2 * SGU_WIDTH * 4)),
        name="sgu",
    )(proj, ln_g.reshape(1, SGU_WIDTH), ln_b.reshape(1, SGU_WIDTH), w_s, b_s.T)


def _merge_kernel(ya_ref, yb_ref, yc_ref, w_ref, ga_ref, gb_ref, gc_ref, o_ref, ybf_ref):
    @pl.when(pl.program_id(1) == 0)
    def _cast_branches():
        for n, y_ref in enumerate((ya_ref, yb_ref, yc_ref)):
            ybf_ref[n] = y_ref[...].astype(BF16)

    acc = None
    for n, gate_ref in enumerate((ga_ref, gb_ref, gc_ref)):
        branch = jnp.dot(ybf_ref[n], w_ref[n], preferred_element_type=F32)
        term = jax.nn.sigmoid(gate_ref[...]) * branch
        acc = term if acc is None else acc + term
    o_ref[...] = acc


def _merge(ya, yb, yc, w_branch, layer, proj, *, tm, tn):
    n = ya.shape[0]
    y_spec = pl.BlockSpec((tm, BRANCH_WIDTH), lambda i, j: (i, 0))

    def gate_spec(b):
        return pl.BlockSpec((tm, tn), lambda i, j: (i, (COL_GATE + b * D_MODEL) // tn + j))

    pipelined = 3 * tm * BRANCH_WIDTH * 4 + 3 * BRANCH_WIDTH * tn * 2 + 4 * tm * tn * 4
    return pl.pallas_call(
        _merge_kernel,
        grid=(n // tm, D_MODEL // tn),
        in_specs=[y_spec, y_spec, y_spec,
                  pl.BlockSpec((None, N_BRANCH, BRANCH_WIDTH, tn), lambda i, j: (layer, 0, 0, j)),
                  gate_spec(0), gate_spec(1), gate_spec(2)],
        out_specs=pl.BlockSpec((tm, tn), lambda i, j: (i, j)),
        out_shape=jax.ShapeDtypeStruct((n, D_MODEL), F32),
        scratch_shapes=[pltpu.VMEM((N_BRANCH, tm, BRANCH_WIDTH), BF16)],
        compiler_params=pltpu.CompilerParams(
            dimension_semantics=("parallel", "arbitrary"),
            vmem_limit_bytes=_vmem_limit(pipelined, 3 * tm * BRANCH_WIDTH * 2 + 4 * tm * tn * 4)),
        name="merge",
    )(ya, yb, yc, w_branch, proj, proj, proj)


def _top_values(s, count):
    rows = []
    for _ in range(count):
        m = jnp.max(s, axis=0, keepdims=True)
        rows.append(m)
        s = jnp.where(s == m, -jnp.inf, s)
    return jnp.concatenate(rows, axis=0)


def _peer_select_kernel(q_ref, keys_ref, c_ref, w1_ref, s2_ref, w2_ref):
    nt = lax.dot_general
    contract_last = (((1,), (1,)), ((), ()))
    for h in range(PEER_HEADS):
        c0 = h * PEER_QDIM
        q1 = q_ref[:, c0:c0 + PEER_HALF].astype(BF16)
        q2 = q_ref[:, c0 + PEER_HALF:c0 + PEER_QDIM].astype(BF16)
        s1 = nt(keys_ref[0, h], q1, contract_last, preferred_element_type=F32)
        s2 = nt(keys_ref[1, h], q2, contract_last, preferred_element_type=F32)
        a = _top_values(s1, PEER_TOPK)
        b = _top_values(s2, PEER_TOPK)
        cand = [a[0:1] + b]
        cand += [a[p:p + 1] + b[0:SUBLANES] for p in range(1, SUBLANES)]
        cand.append(a[SUBLANES:] + b[0:1])
        cand = jnp.concatenate(cand, axis=0)
        work = cand
        for _ in range(PEER_TOPK - 1):
            m = jnp.max(work, axis=0, keepdims=True)
            work = jnp.where(work == m, -jnp.inf, work)
        thr = jnp.max(work, axis=0, keepdims=True)
        top = a[0:1] + b[0:1]
        z = jnp.sum(jnp.where(cand >= thr, jnp.exp(cand - top), 0.0), axis=0, keepdims=True)
        cut = jnp.full(s1.shape, jnp.inf, F32)
        for r in range(PEER_TOPK):
            b_r = b[r:r + 1]
            cut = jnp.minimum(cut, jnp.where(s1 + b_r >= thr, b_r, jnp.inf))
        c_ref[h] = cut[:, None, :]
        w1_ref[h] = jnp.exp(s1 - a[0:1])[:, None, :]
        s2_ref[h] = s2
        w2_ref[h] = jnp.exp(s2 - b[0:1]) / z


def _peer_select(q, keys, layer, *, tm):
    n = q.shape[0]
    rows = jax.ShapeDtypeStruct((PEER_HEADS, N_KEYS, 1, n), F32)
    rows_spec = pl.BlockSpec((PEER_HEADS, N_KEYS, 1, tm), lambda i: (0, 0, 0, i))
    big = jax.ShapeDtypeStruct((PEER_HEADS, N_KEYS, n), F32)
    big_spec = pl.BlockSpec((PEER_HEADS, N_KEYS, tm), lambda i: (0, 0, i))
    return pl.pallas_call(
        _peer_select_kernel,
        grid=(n // tm,),
        in_specs=[pl.BlockSpec((tm, PEER_HEADS * PEER_QDIM), lambda i: (i, 0)),
                  pl.BlockSpec((None, 2, PEER_HEADS, N_KEYS, PEER_HALF), lambda i: (layer, 0, 0, 0, 0))],
        out_specs=[rows_spec, rows_spec, big_spec, big_spec],
        out_shape=[rows, rows, big, big],
        compiler_params=pltpu.CompilerParams(
            dimension_semantics=("parallel",),
            vmem_limit_bytes=_vmem_limit(tm * PEER_HEADS * PEER_QDIM * 4 + 4 * PEER_HEADS * N_KEYS * tm * 4,
                                         16 << 20)),
        name="peer_select",
    )(q, keys)


def _peer_stage(u_ref, vt_ref, c_ref, w1_ref, s2_ref, w2_ref, xn_ref, acc_ref, h_new, h_old):
    te, tm = h_new.shape

    def gate_block(i, c):
        rs = slice(i * N_KEYS, (i + 1) * N_KEYS)
        ls = slice(c * LANES, (c + 1) * LANES)
        gate = None
        for hd in range(PEER_HEADS):
            hit = s2_ref[hd, :, ls] >= c_ref[hd, i, :, ls]
            term = jnp.where(hit, w2_ref[hd, :, ls], 0.0) * w1_ref[hd, i, :, ls]
            gate = term if gate is None else gate + term
        return (_gelu(h_old[rs, ls]) * gate).astype(BF16)

    def output_piece(t):
        cols = [jnp.concatenate([gate_block(i, c) for i in range(te // N_KEYS)], axis=0)
                for c in range(t // LANES, (t + MXU_TILE) // LANES)]
        a = jnp.concatenate(cols, axis=1)
        acc_ref[:, t:t + MXU_TILE] += jnp.dot(vt_ref[...], a, preferred_element_type=F32)

    def hidden_piece(r):
        rs = slice(r, r + MXU_TILE)
        h_new[rs, :] = lax.dot_general(u_ref[rs, :], xn_ref[...], (((1,), (1,)), ((), ())),
                                       preferred_element_type=F32)

    assert te // MXU_TILE == tm // MXU_TILE
    for k in range(tm // MXU_TILE):
        hidden_piece(k * MXU_TILE)
        output_piece(k * MXU_TILE)


def _peer_ffn_kernel(x_ref, g_ref, u_ref, vt_ref, c_ref, w1_ref, s2_ref, w2_ref, o_ref,
                     xn_ref, acc_ref, h0_ref, h1_ref):
    s = pl.program_id(1)

    @pl.when(s == 0)
    def _start_token_tile():
        xn_ref[...] = _rms(x_ref[...], g_ref[...]).astype(BF16)
        for ref in (acc_ref, h0_ref, h1_ref):
            ref[...] = jnp.zeros_like(ref)

    stage = functools.partial(_peer_stage, u_ref, vt_ref, c_ref, w1_ref, s2_ref, w2_ref, xn_ref, acc_ref)

    @pl.when(s % 2 == 0)
    def _even_step():
        stage(h0_ref, h1_ref)

    @pl.when(s % 2 == 1)
    def _odd_step():
        stage(h1_ref, h0_ref)

    @pl.when(s == pl.num_programs(1) - 1)
    def _finish_token_tile():
        o_ref[...] = x_ref[...] + acc_ref[...].T


PEER_PIPELINE_LAG = 1


def _peer_ffn(x, gain, u_bf, vt_bf, layer, cut, w1, s2, w2, *, tm, te):
    n = x.shape[0]
    slabs = te // N_KEYS
    tiles = N_EXPERTS // te
    clip = lambda s, lag: jnp.clip(s - lag, 0, tiles - 1)
    row_spec = pl.BlockSpec((PEER_HEADS, slabs, 1, tm), lambda t, s: (0, clip(s, PEER_PIPELINE_LAG), 0, t))
    full_spec = pl.BlockSpec((PEER_HEADS, N_KEYS, tm), lambda t, s: (0, 0, t))
    pipelined = (2 * tm * D_MODEL * 4 + 2 * te * D_MODEL * 2
                 + 2 * PEER_HEADS * (slabs * SUBLANES + N_KEYS) * tm * 4)
    resident = tm * D_MODEL * 2 + D_MODEL * tm * 4 + 2 * te * tm * 6 + D_MODEL * tm * 4
    return pl.pallas_call(
        _peer_ffn_kernel,
        grid=(n // tm, tiles + PEER_PIPELINE_LAG),
        in_specs=[
            pl.BlockSpec((tm, D_MODEL), lambda t, s: (t, 0)),
            pl.BlockSpec((1, D_MODEL), lambda t, s: (0, 0)),
            pl.BlockSpec((None, te, D_MODEL), lambda t, s: (layer, clip(s, 0), 0)),
            pl.BlockSpec((None, D_MODEL, te), lambda t, s: (layer, 0, clip(s, PEER_PIPELINE_LAG))),
            row_spec, row_spec, full_spec, full_spec,
        ],
        out_specs=pl.BlockSpec((tm, D_MODEL), lambda t, s: (t, 0)),
        out_shape=jax.ShapeDtypeStruct((n, D_MODEL), F32),
        scratch_shapes=[pltpu.VMEM((tm, D_MODEL), BF16),
                        pltpu.VMEM((D_MODEL, tm), F32),
                        pltpu.VMEM((te, tm), F32), pltpu.VMEM((te, tm), F32)],
        compiler_params=pltpu.CompilerParams(
            dimension_semantics=("parallel", "arbitrary"),
            vmem_limit_bytes=_vmem_limit(pipelined, resident)),
        name="peer_ffn",
    )(x, gain.reshape(1, D_MODEL), u_bf, vt_bf, cut, w1, s2, w2)


IN_PROJ_TN = 2 * KV_WIDTH


def _in_proj_out_block(j):
    kv = (SSM_WIDTH + ATTN_WIDTH) // IN_PROJ_TN
    return jnp.where(j < kv, j, jnp.where(j == kv, COL_K // IN_PROJ_TN, j - 1))


def kernel(x, norm1_g, w_in, ssm_a_re, ssm_a_im, ssm_b_re, ssm_b_im, ssm_c_re, ssm_c_im, ssm_d, ssm_log_dt,
           w_glu, b_glu, q_norm_g, k_norm_g, attn_sinks, sgu_ln_g, sgu_ln_b, sgu_w, sgu_b, w_branch, w_out,
           norm2_g, w_query, peer_keys, peer_u, peer_v):
    batch, seq, d_model = x.shape
    assert d_model == D_MODEL and seq % S5_CHUNK == 0
    xt = x.reshape(batch * seq, D_MODEL)
    w_in, w_glu, w_branch, w_out, w_query, peer_keys, peer_u = (
        w.astype(BF16) for w in (w_in, w_glu, w_branch, w_out, w_query, peer_keys, peer_u))
    peer_vt = jnp.swapaxes(peer_v, 1, 2).astype(BF16)
    for l in range(DEPTH):
        proj = _matmul(xt, w_in, l, gain=norm1_g[l], out_block=_in_proj_out_block, tm=1024, tn=IN_PROJ_TN)
        bmat, cmat, pw = _s5_discretise(ssm_a_re[l], ssm_a_im[l], ssm_b_re[l], ssm_b_im[l],
                                        ssm_c_re[l], ssm_c_im[l], ssm_log_dt[l])
        y_ssm = _s5_scan(proj, ssm_d[l], bmat, cmat, pw, batch=batch, seq=seq)
        y_a = _s5_glu(y_ssm, w_glu, l, b_glu[l], tm=512)
        y_b = _swa(proj, q_norm_g[l], k_norm_g[l], attn_sinks[l], batch=batch, seq=seq)
        y_c = _sgu(proj, sgu_ln_g[l], sgu_ln_b[l], sgu_w[l], sgu_b[l], tm=256)
        merged = _merge(y_a, y_b, y_c, w_branch, l, proj, tm=512, tn=512)
        xt = _matmul(merged, w_out, l, residual=xt, tm=1024, tn=512)
        q = _matmul(xt, w_query, l, gain=norm2_g[l], tm=1024, tn=512)
        cut, w1, s2, w2 = _peer_select(q, peer_keys, l, tm=256)
        xt = _peer_ffn(xt, norm2_g[l], peer_u, peer_vt, l, cut, w1, s2, w2, tm=512, te=512)
    return xt.reshape(batch, seq, D_MODEL)
```

```python
import functools
import math

import jax
import jax.numpy as jnp
from jax import lax
from jax.experimental import pallas as pl
from jax.experimental.pallas import tpu as pltpu

F32 = jnp.float32
BF16 = jnp.bfloat16

D_MODEL = 2048
DEPTH = 2
SSM_WIDTH = 1024
SSM_GROUP = 16
SSM_GROUPS = SSM_WIDTH // SSM_GROUP
SSM_STATE = 64
HEAD_DIM = 64
N_Q_HEADS = 16
N_KV_HEADS = 4
Q_PER_KV = N_Q_HEADS // N_KV_HEADS
WINDOW = 128
ATTN_WIDTH = N_Q_HEADS * HEAD_DIM
KV_WIDTH = N_KV_HEADS * HEAD_DIM
SGU_WIDTH = 1024
SGU_HEADS = 8
SGU_HEAD_DIM = SGU_WIDTH // SGU_HEADS
SGU_CHUNK = 128
N_BRANCH = 3
BRANCH_WIDTH = 1024
PEER_HEADS = 8
PEER_QDIM = 256
PEER_HALF = PEER_QDIM // 2
N_KEYS = 128
N_EXPERTS = N_KEYS * N_KEYS
PEER_TOPK = 16
EPS = 1e-6

V7X_VMEM_BYTES = 64 * 1024 * 1024
SUBLANES = 8
LANES = 128
BF16_ROWS = 2 * SUBLANES
MXU_TILE = 256

COL_SSM = 0
COL_Q = COL_SSM + SSM_WIDTH
COL_SGU = COL_Q + ATTN_WIDTH
COL_GATE = COL_SGU + 2 * SGU_WIDTH
COL_K = COL_GATE + N_BRANCH * D_MODEL
COL_V = COL_K + KV_WIDTH
IN_COLS = COL_V + KV_WIDTH

S5_GB = 16
S5_NGB = SSM_GROUPS // S5_GB
S5_IN = S5_GB * SSM_GROUP
S5_ST = S5_GB * SSM_STATE
S5_CHUNK = 256

SQRT_HALF = math.sqrt(0.5)


def _vmem_limit(pipelined_bytes, resident_bytes):
    need = 2 * pipelined_bytes + resident_bytes + (4 << 20)
    return int(min(need, V7X_VMEM_BYTES - (8 << 20)))


def _gelu(x):
    return 0.5 * x * (1.0 + lax.erf(x * SQRT_HALF))


def _rms(x, g):
    return x * lax.rsqrt(jnp.mean(x * x, axis=-1, keepdims=True) + EPS) * g


def _matmul_kernel(*refs, norm, residual):
    refs = list(refs)
    a_ref = refs.pop(0)
    g_ref = refs.pop(0) if norm else None
    w_ref = refs.pop(0)
    r_ref = refs.pop(0) if residual else None
    o_ref, abf_ref = refs

    @pl.when(pl.program_id(1) == 0)
    def _prepare_lhs():
        a = a_ref[...]
        if norm:
            a = _rms(a, g_ref[...])
        abf_ref[...] = a.astype(BF16)

    acc = jnp.dot(abf_ref[...], w_ref[...], preferred_element_type=F32)
    if residual:
        acc = r_ref[...] + acc
    o_ref[...] = acc


def _matmul(a, w, layer, *, gain=None, residual=None, out_block=lambda j: j, tm, tn):
    m, k = a.shape
    n = w.shape[2]
    assert m % tm == 0 and n % tn == 0
    operands = [a]
    in_specs = [pl.BlockSpec((tm, k), lambda i, j: (i, 0))]
    if gain is not None:
        operands.append(gain.reshape(1, k))
        in_specs.append(pl.BlockSpec((1, k), lambda i, j: (0, 0)))
    operands.append(w)
    in_specs.append(pl.BlockSpec((None, k, tn), lambda i, j: (layer, 0, j)))
    if residual is not None:
        operands.append(residual)
        in_specs.append(pl.BlockSpec((tm, tn), lambda i, j: (i, j)))
    pipelined = tm * k * 4 + k * tn * 2 + tm * tn * 4 * (2 if residual is not None else 1)
    return pl.pallas_call(
        functools.partial(_matmul_kernel, norm=gain is not None, residual=residual is not None),
        grid=(m // tm, n // tn),
        in_specs=in_specs,
        out_specs=pl.BlockSpec((tm, tn), lambda i, j: (i, out_block(j))),
        out_shape=jax.ShapeDtypeStruct((m, n), F32),
        scratch_shapes=[pltpu.VMEM((tm, k), BF16)],
        compiler_params=pltpu.CompilerParams(
            dimension_semantics=("parallel", "arbitrary"),
            vmem_limit_bytes=_vmem_limit(pipelined, tm * k * 2 + tm * k * 4)),
        name="matmul",
    )(*operands)


def _s5_kernel(u_ref, d_ref, bm_ref, cm_ref, pw_ref, o_ref, x_ref, carry_ref):
    tc = u_ref.shape[0]
    ng = tc // SUBLANES

    @pl.when(pl.program_id(2) == 0)
    def _reset_state():
        carry_ref[...] = jnp.zeros_like(carry_ref)

    u = u_ref[...]
    bu = jnp.dot(u.astype(BF16), bm_ref[0], preferred_element_type=F32)
    re = bu[:, :S5_ST].reshape(ng, SUBLANES, S5_ST)
    im = bu[:, S5_ST:].reshape(ng, SUBLANES, S5_ST)
    for idx, k in enumerate((1, 2, 4)):
        lre = pw_ref[0, 2 * idx]
        lim = pw_ref[0, 2 * idx + 1]
        sre = pltpu.roll(re, k, 1)
        sim = pltpu.roll(im, k, 1)
        re, im = re + (lre * sre - lim * sim), im + (lre * sim + lim * sre)
    x_ref[:, :S5_ST] = re.reshape(tc, S5_ST)
    x_ref[:, S5_ST:] = im.reshape(tc, S5_ST)

    cyre = pw_ref[0, 6]
    cyim = pw_ref[0, 7]

    def _carry_group(g, carry):
        cr, ci = carry
        r0 = pl.multiple_of(g * SUBLANES, SUBLANES)
        xr = x_ref[pl.ds(r0, SUBLANES), :S5_ST] + (cyre * cr - cyim * ci)
        xi = x_ref[pl.ds(r0, SUBLANES), S5_ST:] + (cyre * ci + cyim * cr)
        x_ref[pl.ds(r0, SUBLANES), :S5_ST] = xr
        x_ref[pl.ds(r0, SUBLANES), S5_ST:] = xi
        last = SUBLANES - 1
        return (jnp.broadcast_to(xr[last:, :], (SUBLANES, S5_ST)),
                jnp.broadcast_to(xi[last:, :], (SUBLANES, S5_ST)))

    cr, ci = lax.fori_loop(0, ng, _carry_group, (carry_ref[0], carry_ref[1]))
    carry_ref[0] = cr
    carry_ref[1] = ci

    y = jnp.dot(x_ref[...].astype(BF16), cm_ref[0], preferred_element_type=F32)
    o_ref[...] = y + d_ref[...] * u


def _s5_discretise(a_re, a_im, b_re, b_im, c_re, c_im, log_dt):
    dt = jnp.exp(log_dt)[:, None]

    def a_pow(k):
        mag = jnp.exp(a_re * dt * k)
        return mag * jnp.cos(a_im * dt * k), mag * jnp.sin(a_im * dt * k)

    abar_re, abar_im = a_pow(1.0)
    num_re, num_im = abar_re - 1.0, abar_im
    den = a_re * a_re + a_im * a_im
    coef_re = (num_re * a_re + num_im * a_im) / den
    coef_im = (num_im * a_re - num_re * a_im) / den
    bbar_re = coef_re[..., None] * b_re - coef_im[..., None] * b_im
    bbar_im = coef_re[..., None] * b_im + coef_im[..., None] * b_re

    eye = jnp.eye(S5_GB, dtype=F32)

    def in_block(b):
        b = b.reshape(S5_NGB, S5_GB, SSM_STATE, SSM_GROUP)
        return jnp.einsum('bgph,gk->bghkp', b, eye).reshape(S5_NGB, S5_IN, S5_ST)

    def out_block(c):
        c = c.reshape(S5_NGB, S5_GB, SSM_GROUP, SSM_STATE)
        return jnp.einsum('bghp,gk->bgpkh', c, eye).reshape(S5_NGB, S5_ST, S5_IN)

    bmat = jnp.concatenate([in_block(bbar_re), in_block(bbar_im)], axis=2).astype(BF16)
    cmat = jnp.concatenate([out_block(c_re), out_block(-c_im)], axis=1).astype(BF16)

    rows = jnp.arange(SUBLANES)[:, None, None]

    def tile(vals):
        return vals.reshape(SUBLANES, S5_NGB, S5_ST).transpose(1, 0, 2)

    tiles = []
    for k in (1, 2, 4):
        pr, pi = a_pow(float(k))
        tiles.append(tile(jnp.where(rows >= k, pr[None], 0.0)))
        tiles.append(tile(jnp.where(rows >= k, pi[None], 0.0)))
    steps = (rows + 1).astype(F32)
    mag = jnp.exp(a_re[None] * dt[None] * steps)
    tiles.append(tile(mag * jnp.cos(a_im[None] * dt[None] * steps)))
    tiles.append(tile(mag * jnp.sin(a_im[None] * dt[None] * steps)))
    pw = jnp.stack(tiles, axis=1)
    return bmat, cmat, pw


def _s5_scan(proj, ssm_d, bmat, cmat, pw, *, batch, seq):
    n = batch * seq
    nc = seq // S5_CHUNK
    pipelined = S5_CHUNK * S5_IN * 4 * 2 + S5_IN * 2 * S5_ST * 2 * 2 + 8 * SUBLANES * S5_ST * 4
    resident = S5_CHUNK * 2 * S5_ST * 4 * 6
    return pl.pallas_call(
        _s5_kernel,
        grid=(S5_NGB, batch, nc),
        in_specs=[
            pl.BlockSpec((S5_CHUNK, S5_IN), lambda g, b, c: (b * nc + c, COL_SSM // S5_IN + g)),
            pl.BlockSpec((1, S5_IN), lambda g, b, c: (0, g)),
            pl.BlockSpec((1, S5_IN, 2 * S5_ST), lambda g, b, c: (g, 0, 0)),
            pl.BlockSpec((1, 2 * S5_ST, S5_IN), lambda g, b, c: (g, 0, 0)),
            pl.BlockSpec((1, 8, SUBLANES, S5_ST), lambda g, b, c: (g, 0, 0, 0)),
        ],
        out_specs=pl.BlockSpec((S5_CHUNK, S5_IN), lambda g, b, c: (b * nc + c, g)),
        out_shape=jax.ShapeDtypeStruct((n, SSM_WIDTH), F32),
        scratch_shapes=[pltpu.VMEM((S5_CHUNK, 2 * S5_ST), F32),
                        pltpu.VMEM((2, SUBLANES, S5_ST), F32)],
        compiler_params=pltpu.CompilerParams(
            dimension_semantics=("parallel", "parallel", "arbitrary"),
            vmem_limit_bytes=_vmem_limit(pipelined, resident)),
        name="s5_scan",
    )(proj, ssm_d.reshape(1, SSM_WIDTH), bmat, cmat, pw)


def _s5_glu_kernel(y_ref, w_ref, b_ref, o_ref):
    y = _gelu(y_ref[...])
    z = jnp.dot(y.astype(BF16), w_ref[...], preferred_element_type=F32) + b_ref[...]
    o_ref[...] = y * jax.nn.sigmoid(z)


def _s5_glu(y, w_glu, layer, b_glu, *, tm):
    n = y.shape[0]
    return pl.pallas_call(
        _s5_glu_kernel,
        grid=(n // tm,),
        in_specs=[pl.BlockSpec((tm, SSM_WIDTH), lambda i: (i, 0)),
                  pl.BlockSpec((None, SSM_WIDTH, SSM_WIDTH), lambda i: (layer, 0, 0)),
                  pl.BlockSpec((1, SSM_WIDTH), lambda i: (0, 0))],
        out_specs=pl.BlockSpec((tm, SSM_WIDTH), lambda i: (i, 0)),
        out_shape=jax.ShapeDtypeStruct((n, SSM_WIDTH), F32),
        compiler_params=pltpu.CompilerParams(
            dimension_semantics=("parallel",),
            vmem_limit_bytes=_vmem_limit(2 * tm * SSM_WIDTH * 4 + SSM_WIDTH * SSM_WIDTH * 2,
                                         4 * tm * SSM_WIDTH * 4)),
        name="s5_glu",
    )(y, w_glu, b_glu.reshape(1, SSM_WIDTH))


def _swa_kernel(sink_ref, q_ref, kp_ref, kc_ref, vp_ref, vc_ref, qg_ref, kg_ref, o_ref, *, blocks_per_seq):
    first = (pl.program_id(0) % blocks_per_seq) == 0
    q = q_ref[...]
    k = jnp.concatenate([kp_ref[...], kc_ref[...]], axis=0)
    v = jnp.concatenate([vp_ref[...], vc_ref[...]], axis=0)
    rows = Q_PER_KV * WINDOW
    qpos = (lax.broadcasted_iota(jnp.int32, (rows, 2 * WINDOW), 0) & (WINDOW - 1)) + WINDOW
    kpos = lax.broadcasted_iota(jnp.int32, (rows, 2 * WINDOW), 1)
    diff = qpos - kpos
    first_key = jnp.where(first, WINDOW, 0)
    valid = (diff >= 0) & (diff < WINDOW) & (kpos >= first_key)
    outs = []
    for kv in range(N_KV_HEADS):
        ks = slice(kv * HEAD_DIM, (kv + 1) * HEAD_DIM)
        k_h = _rms(k[:, ks], kg_ref[...]).astype(BF16)
        v_h = v[:, ks].astype(BF16)
        heads = range(kv * Q_PER_KV, (kv + 1) * Q_PER_KV)
        q_grp = jnp.concatenate([q[:, h * HEAD_DIM:(h + 1) * HEAD_DIM] for h in heads], axis=0)
        q_grp = _rms(q_grp, qg_ref[...]).astype(BF16)
        sink = jnp.concatenate([jnp.full((WINDOW, 1), sink_ref[h], F32) for h in heads], axis=0)
        s = lax.dot_general(q_grp, k_h, (((1,), (1,)), ((), ())), preferred_element_type=F32)
        s = jnp.where(valid, s * (HEAD_DIM ** -0.5), -1e30)
        m = jnp.maximum(jnp.max(s, axis=-1, keepdims=True), sink)
        p = jnp.exp(s - m)
        denom = jnp.sum(p, axis=-1, keepdims=True) + jnp.exp(sink - m)
        o_grp = jnp.dot(p.astype(BF16), v_h, preferred_element_type=F32) / denom
        outs += [o_grp[g * WINDOW:(g + 1) * WINDOW] for g in range(Q_PER_KV)]
    o_ref[...] = jnp.concatenate(outs, axis=-1)


def _swa(proj, q_g, k_g, sinks, *, batch, seq):
    n = batch * seq
    nb = seq // WINDOW
    prev = lambda r: jnp.maximum(r - 1, 0)
    pipelined = WINDOW * (2 * ATTN_WIDTH + 4 * KV_WIDTH) * 4
    return pl.pallas_call(
        functools.partial(_swa_kernel, blocks_per_seq=nb),
        grid=(n // WINDOW,),
        in_specs=[
            pl.BlockSpec(memory_space=pltpu.SMEM),
            pl.BlockSpec((WINDOW, ATTN_WIDTH), lambda r: (r, COL_Q // ATTN_WIDTH)),
            pl.BlockSpec((WINDOW, KV_WIDTH), lambda r: (prev(r), COL_K // KV_WIDTH)),
            pl.BlockSpec((WINDOW, KV_WIDTH), lambda r: (r, COL_K // KV_WIDTH)),
            pl.BlockSpec((WINDOW, KV_WIDTH), lambda r: (prev(r), COL_V // KV_WIDTH)),
            pl.BlockSpec((WINDOW, KV_WIDTH), lambda r: (r, COL_V // KV_WIDTH)),
            pl.BlockSpec((1, HEAD_DIM), lambda r: (0, 0)),
            pl.BlockSpec((1, HEAD_DIM), lambda r: (0, 0)),
        ],
        out_specs=pl.BlockSpec((WINDOW, ATTN_WIDTH), lambda r: (r, 0)),
        out_shape=jax.ShapeDtypeStruct((n, ATTN_WIDTH), F32),
        compiler_params=pltpu.CompilerParams(
            dimension_semantics=("parallel",),
            vmem_limit_bytes=_vmem_limit(pipelined, 16 << 20)),
        name="swa",
    )(sinks, proj, proj, proj, proj, proj, q_g.reshape(1, HEAD_DIM), k_g.reshape(1, HEAD_DIM))


def _sgu_kernel(z_ref, g_ref, b_ref, w_ref, bs_ref, o_ref):
    tm = z_ref.shape[0]
    z = _gelu(z_ref[...])
    u = z[:, :SGU_WIDTH]
    v = z[:, SGU_WIDTH:]
    vc = v - jnp.mean(v, axis=-1, keepdims=True)
    vn = vc * lax.rsqrt(jnp.mean(vc * vc, axis=-1, keepdims=True) + EPS) * g_ref[...] + b_ref[...]
    vn = vn.astype(BF16)
    row = lax.broadcasted_iota(jnp.int32, (SGU_CHUNK, SGU_CHUNK), 0)
    col = lax.broadcasted_iota(jnp.int32, (SGU_CHUNK, SGU_CHUNK), 1)
    for h in range(SGU_HEADS):
        w_h = jnp.where(row >= col, w_ref[h], 0.0).astype(BF16)
        cs = slice(h * SGU_HEAD_DIM, (h + 1) * SGU_HEAD_DIM)
        for c in range(tm // SGU_CHUNK):
            rs = slice(c * SGU_CHUNK, (c + 1) * SGU_CHUNK)
            mixed = jnp.dot(w_h, vn[rs, cs], preferred_element_type=F32) + bs_ref[:, h:h + 1]
            o_ref[rs, cs] = u[rs, cs] * mixed


def _sgu(proj, ln_g, ln_b, w_s, b_s, *, tm):
    n = proj.shape[0]
    return pl.pallas_call(
        _sgu_kernel,
        grid=(n // tm,),
        in_specs=[
            pl.BlockSpec((tm, 2 * SGU_WIDTH), lambda i: (i, COL_SGU // (2 * SGU_WIDTH))),
            pl.BlockSpec((1, SGU_WIDTH), lambda i: (0, 0)),
            pl.BlockSpec((1, SGU_WIDTH), lambda i: (0, 0)),
            pl.BlockSpec((SGU_HEADS, SGU_CHUNK, SGU_CHUNK), lambda i: (0, 0, 0)),
            pl.BlockSpec((SGU_CHUNK, SGU_HEADS), lambda i: (0, 0)),
        ],
        out_specs=pl.BlockSpec((tm, SGU_WIDTH), lambda i: (i, 0)),
        out_shape=jax.ShapeDtypeStruct((n, SGU_WIDTH), F32),
        compiler_params=pltpu.CompilerParams(
            dimension_semantics=("parallel",),
            vmem_limit_bytes=_vmem_limit(tm * 3 * SGU_WIDTH * 4, 6 * tm * 2 * SGU_WIDTH * 4)),
        name="sgu",
    )(proj, ln_g.reshape(1, SGU_WIDTH), ln_b.reshape(1, SGU_WIDTH), w_s, b_s.T)


def _merge_kernel(ya_ref, yb_ref, yc_ref, w_ref, ga_ref, gb_ref, gc_ref, o_ref, ybf_ref):
    @pl.when(pl.program_id(1) == 0)
    def _cast_branches():
        for n, y_ref in enumerate((ya_ref, yb_ref, yc_ref)):
            ybf_ref[n] = y_ref[...].astype(BF16)

    acc = None
    for n, gate_ref in enumerate((ga_ref, gb_ref, gc_ref)):
        branch = jnp.dot(ybf_ref[n], w_ref[n], preferred_element_type=F32)
        term = jax.nn.sigmoid(gate_ref[...]) * branch
        acc = term if acc is None else acc + term
    o_ref[...] = acc


def _merge(ya, yb, yc, w_branch, layer, proj, *, tm, tn):
    n = ya.shape[0]
    y_spec = pl.BlockSpec((tm, BRANCH_WIDTH), lambda i, j: (i, 0))

    def gate_spec(b):
        return pl.BlockSpec((tm, tn), lambda i, j: (i, (COL_GATE + b * D_MODEL) // tn + j))

    pipelined = 3 * tm * BRANCH_WIDTH * 4 + 3 * BRANCH_WIDTH * tn * 2 + 4 * tm * tn * 4
    return pl.pallas_call(
        _merge_kernel,
        grid=(n // tm, D_MODEL // tn),
        in_specs=[y_spec, y_spec, y_spec,
                  pl.BlockSpec((None, N_BRANCH, BRANCH_WIDTH, tn), lambda i, j: (layer, 0, 0, j)),
                  gate_spec(0), gate_spec(1), gate_spec(2)],
        out_specs=pl.BlockSpec((tm, tn), lambda i, j: (i, j)),
        out_shape=jax.ShapeDtypeStruct((n, D_MODEL), F32),
        scratch_shapes=[pltpu.VMEM((N_BRANCH, tm, BRANCH_WIDTH), BF16)],
        compiler_params=pltpu.CompilerParams(
            dimension_semantics=("parallel", "arbitrary"),
            vmem_limit_bytes=_vmem_limit(pipelined, 3 * tm * BRANCH_WIDTH * 2 + 4 * tm * tn * 4)),
        name="merge",
    )(ya, yb, yc, w_branch, proj, proj, proj)


def _top_values(s, count):
    rows = []
    rank = jnp.full(s.shape, float(count), F32)
    for r in range(count):
        m = jnp.max(s, axis=0, keepdims=True)
        rows.append(m)
        hit = s == m
        rank = jnp.where(hit, float(r), rank)
        s = jnp.where(hit, -jnp.inf, s)
    return jnp.concatenate(rows, axis=0), rank


def _peer_select_kernel(q_ref, keys_ref, n_ref, w1_ref, r2_ref, w2_ref):
    nt = lax.dot_general
    contract_last = (((1,), (1,)), ((), ()))
    for h in range(PEER_HEADS):
        c0 = h * PEER_QDIM
        q1 = q_ref[:, c0:c0 + PEER_HALF].astype(BF16)
        q2 = q_ref[:, c0 + PEER_HALF:c0 + PEER_QDIM].astype(BF16)
        s1 = nt(keys_ref[0, h], q1, contract_last, preferred_element_type=F32)
        s2 = nt(keys_ref[1, h], q2, contract_last, preferred_element_type=F32)
        a, _ = _top_values(s1, PEER_TOPK)
        b, rank2 = _top_values(s2, PEER_TOPK)
        cand = [a[0:1] + b]
        cand += [a[p:p + 1] + b[0:SUBLANES] for p in range(1, SUBLANES)]
        cand.append(a[SUBLANES:] + b[0:1])
        cand = jnp.concatenate(cand, axis=0)
        work = cand
        for _ in range(PEER_TOPK - 1):
            m = jnp.max(work, axis=0, keepdims=True)
            work = jnp.where(work == m, -jnp.inf, work)
        thr = jnp.max(work, axis=0, keepdims=True)
        top = a[0:1] + b[0:1]
        z = jnp.sum(jnp.where(cand >= thr, jnp.exp(cand - top), 0.0), axis=0, keepdims=True)
        count = jnp.zeros(s1.shape, F32)
        for r in range(PEER_TOPK):
            count = count + jnp.where(s1 + b[r:r + 1] >= thr, 1.0, 0.0)
        n_ref[h] = count[:, None, :]
        w1_ref[h] = jnp.exp(s1 - a[0:1])[:, None, :]
        packed = (N_KEYS // BF16_ROWS, BF16_ROWS, s2.shape[1])
        r2_ref[h] = rank2.astype(BF16).reshape(packed)
        w2_ref[h] = (jnp.exp(s2 - b[0:1]) / z).astype(BF16).reshape(packed)


def _peer_select(q, keys, layer, *, tm):
    n = q.shape[0]
    rows = jax.ShapeDtypeStruct((PEER_HEADS, N_KEYS, 1, n), F32)
    rows_spec = pl.BlockSpec((PEER_HEADS, N_KEYS, 1, tm), lambda i: (0, 0, 0, i))
    big = jax.ShapeDtypeStruct((PEER_HEADS, N_KEYS // BF16_ROWS, BF16_ROWS, n), BF16)
    big_spec = pl.BlockSpec((PEER_HEADS, N_KEYS // BF16_ROWS, BF16_ROWS, tm), lambda i: (0, 0, 0, i))
    return pl.pallas_call(
        _peer_select_kernel,
        grid=(n // tm,),
        in_specs=[pl.BlockSpec((tm, PEER_HEADS * PEER_QDIM), lambda i: (i, 0)),
                  pl.BlockSpec((None, 2, PEER_HEADS, N_KEYS, PEER_HALF), lambda i: (layer, 0, 0, 0, 0))],
        out_specs=[rows_spec, rows_spec, big_spec, big_spec],
        out_shape=[rows, rows, big, big],
        compiler_params=pltpu.CompilerParams(
            dimension_semantics=("parallel",),
            vmem_limit_bytes=_vmem_limit(tm * PEER_HEADS * PEER_QDIM * 4 + 4 * PEER_HEADS * N_KEYS * tm * 4,
                                         16 << 20)),
        name="peer_select",
    )(q, keys)


def _peer_stage(u_ref, vt_ref, n_ref, w1_ref, r2_ref, w2_ref, xn_ref, acc_ref, h_new, h_old):
    te, tm = h_new.shape
    packed = (N_KEYS // BF16_ROWS, BF16_ROWS, LANES)

    def row(ref, hd, i, ls):
        return jnp.broadcast_to(ref[hd, i, :, ls], (BF16_ROWS, LANES)).astype(BF16)

    def gate_block(i, c):
        rs = slice(i * N_KEYS, (i + 1) * N_KEYS)
        ls = slice(c * LANES, (c + 1) * LANES)
        gate = None
        for hd in range(PEER_HEADS):
            margin = jnp.maximum(row(n_ref, hd, i, ls) - r2_ref[hd, :, :, ls], 0.0)
            term = jnp.minimum(margin, w2_ref[hd, :, :, ls]) * row(w1_ref, hd, i, ls)
            gate = term if gate is None else gate + term
        act = _gelu(h_old[rs, ls]).reshape(packed).astype(BF16) * gate
        return act.reshape(N_KEYS, LANES)

    def output_piece(t):
        cols = [jnp.concatenate([gate_block(i, c) for i in range(te // N_KEYS)], axis=0)
                for c in range(t // LANES, (t + MXU_TILE) // LANES)]
        a = jnp.concatenate(cols, axis=1)
        acc_ref[:, t:t + MXU_TILE] += jnp.dot(vt_ref[...], a, preferred_element_type=F32)

    def hidden_piece(r):
        rs = slice(r, r + MXU_TILE)
        h_new[rs, :] = lax.dot_general(u_ref[rs, :], xn_ref[...], (((1,), (1,)), ((), ())),
                                       preferred_element_type=F32)

    hidden_per_output = (te // MXU_TILE) // (tm // MXU_TILE)
    assert hidden_per_output * tm == te
    for k in range(tm // MXU_TILE):
        for r in range(hidden_per_output):
            hidden_piece((k * hidden_per_output + r) * MXU_TILE)
        output_piece(k * MXU_TILE)


def _peer_ffn_kernel(x_ref, g_ref, u_ref, vt_ref, n_ref, w1_ref, r2_ref, w2_ref, o_ref,
                     xn_ref, acc_ref, h0_ref, h1_ref):
    s = pl.program_id(1)

    @pl.when(s == 0)
    def _start_token_tile():
        xn_ref[...] = _rms(x_ref[...], g_ref[...]).astype(BF16)
        for ref in (acc_ref, h0_ref, h1_ref):
            ref[...] = jnp.zeros_like(ref)

    stage = functools.partial(_peer_stage, u_ref, vt_ref, n_ref, w1_ref, r2_ref, w2_ref, xn_ref, acc_ref)

    @pl.when(s % 2 == 0)
    def _even_step():
        stage(h0_ref, h1_ref)

    @pl.when(s % 2 == 1)
    def _odd_step():
        stage(h1_ref, h0_ref)

    @pl.when(s == pl.num_programs(1) - 1)
    def _finish_token_tile():
        o_ref[...] = x_ref[...] + acc_ref[...].T


PEER_PIPELINE_LAG = 1


def _peer_ffn(x, gain, u_bf, vt_bf, layer, count, w1, rank2, w2, *, tm, te):
    n = x.shape[0]
    slabs = te // N_KEYS
    tiles = N_EXPERTS // te
    clip = lambda s, lag: jnp.clip(s - lag, 0, tiles - 1)
    row_spec = pl.BlockSpec((PEER_HEADS, slabs, 1, tm), lambda t, s: (0, clip(s, PEER_PIPELINE_LAG), 0, t))
    full_spec = pl.BlockSpec((PEER_HEADS, N_KEYS // BF16_ROWS, BF16_ROWS, tm), lambda t, s: (0, 0, 0, t))
    pipelined = (2 * tm * D_MODEL * 4 + 2 * te * D_MODEL * 2
                 + 2 * PEER_HEADS * (slabs * SUBLANES * 4 + N_KEYS * 2) * tm)
    resident = tm * D_MODEL * 2 + D_MODEL * tm * 4 + 2 * te * tm * 4 + D_MODEL * tm * 4
    return pl.pallas_call(
        _peer_ffn_kernel,
        grid=(n // tm, tiles + PEER_PIPELINE_LAG),
        in_specs=[
            pl.BlockSpec((tm, D_MODEL), lambda t, s: (t, 0)),
            pl.BlockSpec((1, D_MODEL), lambda t, s: (0, 0)),
            pl.BlockSpec((None, te, D_MODEL), lambda t, s: (layer, clip(s, 0), 0)),
            pl.BlockSpec((None, D_MODEL, te), lambda t, s: (layer, 0, clip(s, PEER_PIPELINE_LAG))),
            row_spec, row_spec, full_spec, full_spec,
        ],
        out_specs=pl.BlockSpec((tm, D_MODEL), lambda t, s: (t, 0)),
        out_shape=jax.ShapeDtypeStruct((n, D_MODEL), F32),
        scratch_shapes=[pltpu.VMEM((tm, D_MODEL), BF16),
                        pltpu.VMEM((D_MODEL, tm), F32),
                        pltpu.VMEM((te, tm), F32), pltpu.VMEM((te, tm), F32)],
        compiler_params=pltpu.CompilerParams(
            dimension_semantics=("parallel", "arbitrary"),
            vmem_limit_bytes=_vmem_limit(pipelined, resident)),
        name="peer_ffn",
    )(x, gain.reshape(1, D_MODEL), u_bf, vt_bf, count, w1, rank2, w2)


IN_PROJ_TN = 2 * KV_WIDTH


def _in_proj_out_block(j):
    kv = (SSM_WIDTH + ATTN_WIDTH) // IN_PROJ_TN
    return jnp.where(j < kv, j, jnp.where(j == kv, COL_K // IN_PROJ_TN, j - 1))


def kernel(x, norm1_g, w_in, ssm_a_re, ssm_a_im, ssm_b_re, ssm_b_im, ssm_c_re, ssm_c_im, ssm_d, ssm_log_dt,
           w_glu, b_glu, q_norm_g, k_norm_g, attn_sinks, sgu_ln_g, sgu_ln_b, sgu_w, sgu_b, w_branch, w_out,
           norm2_g, w_query, peer_keys, peer_u, peer_v):
    batch, seq, d_model = x.shape
    assert d_model == D_MODEL and seq % S5_CHUNK == 0
    xt = x.reshape(batch * seq, D_MODEL)
    w_in, w_glu, w_branch, w_out, w_query, peer_keys, peer_u = (
        w.astype(BF16) for w in (w_in, w_glu, w_branch, w_out, w_query, peer_keys, peer_u))
    peer_vt = jnp.swapaxes(peer_v, 1, 2).astype(BF16)
    for l in range(DEPTH):
        proj = _matmul(xt, w_in, l, gain=norm1_g[l], out_block=_in_proj_out_block, tm=1024, tn=IN_PROJ_TN)
        bmat, cmat, pw = _s5_discretise(ssm_a_re[l], ssm_a_im[l], ssm_b_re[l], ssm_b_im[l],
                                        ssm_c_re[l], ssm_c_im[l], ssm_log_dt[l])
        y_ssm = _s5_scan(proj, ssm_d[l], bmat, cmat, pw, batch=batch, seq=seq)
        y_a = _s5_glu(y_ssm, w_glu, l, b_glu[l], tm=512)
        y_b = _swa(proj, q_norm_g[l], k_norm_g[l], attn_sinks[l], batch=batch, seq=seq)
        y_c = _sgu(proj, sgu_ln_g[l], sgu_ln_b[l], sgu_w[l], sgu_b[l], tm=256)
        merged = _merge(y_a, y_b, y_c, w_branch, l, proj, tm=512, tn=512)
        xt = _matmul(merged, w_out, l, residual=xt, tm=1024, tn=512)
        q = _matmul(xt, w_query, l, gain=norm2_g[l], tm=1024, tn=512)
        count, w1, rank2, w2 = _peer_select(q, peer_keys, l, tm=256)
        xt = _peer_ffn(xt, norm2_g[l], peer_u, peer_vt, l, count, w1, rank2, w2, tm=512, te=1024)
    return xt.reshape(batch, seq, D_MODEL)
```

```python
import functools
import math

import jax
import jax.numpy as jnp
from jax import lax
from jax.experimental import pallas as pl
from jax.experimental.pallas import tpu as pltpu

F32 = jnp.float32
BF16 = jnp.bfloat16

D_MODEL = 2048
DEPTH = 2
SSM_WIDTH = 1024
SSM_GROUP = 16
SSM_GROUPS = SSM_WIDTH // SSM_GROUP
SSM_STATE = 64
HEAD_DIM = 64
N_Q_HEADS = 16
N_KV_HEADS = 4
Q_PER_KV = N_Q_HEADS // N_KV_HEADS
WINDOW = 128
ATTN_WIDTH = N_Q_HEADS * HEAD_DIM
KV_WIDTH = N_KV_HEADS * HEAD_DIM
SGU_WIDTH = 1024
SGU_HEADS = 8
SGU_HEAD_DIM = SGU_WIDTH // SGU_HEADS
SGU_CHUNK = 128
N_BRANCH = 3
BRANCH_WIDTH = 1024
PEER_HEADS = 8
PEER_QDIM = 256
PEER_HALF = PEER_QDIM // 2
N_KEYS = 128
N_EXPERTS = N_KEYS * N_KEYS
PEER_TOPK = 16
EPS = 1e-6

V7X_VMEM_BYTES = 64 * 1024 * 1024
SUBLANES = 8
LANES = 128
BF16_ROWS = 2 * SUBLANES
MXU_TILE = 256

COL_SSM = 0
COL_Q = COL_SSM + SSM_WIDTH
COL_SGU = COL_Q + ATTN_WIDTH
COL_GATE = COL_SGU + 2 * SGU_WIDTH
COL_K = COL_GATE + N_BRANCH * D_MODEL
COL_V = COL_K + KV_WIDTH
IN_COLS = COL_V + KV_WIDTH

S5_GB = 16
S5_NGB = SSM_GROUPS // S5_GB
S5_IN = S5_GB * SSM_GROUP
S5_ST = S5_GB * SSM_STATE
S5_CHUNK = 512

SQRT_HALF = math.sqrt(0.5)


def _vmem_limit(pipelined_bytes, resident_bytes):
    need = 2 * pipelined_bytes + resident_bytes + (4 << 20)
    return int(min(need, V7X_VMEM_BYTES - (8 << 20)))


def _gelu(x):
    return 0.5 * x * (1.0 + lax.erf(x * SQRT_HALF))


def _rms(x, g):
    return x * lax.rsqrt(jnp.mean(x * x, axis=-1, keepdims=True) + EPS) * g


def _matmul_kernel(*refs, norm, residual):
    refs = list(refs)
    a_ref = refs.pop(0)
    g_ref = refs.pop(0) if norm else None
    w_ref = refs.pop(0)
    r_ref = refs.pop(0) if residual else None
    o_ref, abf_ref = refs

    @pl.when(pl.program_id(1) == 0)
    def _prepare_lhs():
        a = a_ref[...]
        if norm:
            a = _rms(a, g_ref[...])
        abf_ref[...] = a.astype(BF16)

    acc = jnp.dot(abf_ref[...], w_ref[...], preferred_element_type=F32)
    if residual:
        acc = r_ref[...] + acc
    o_ref[...] = acc


def _matmul(a, w, layer, *, gain=None, residual=None, out_block=lambda j: j, tm, tn):
    m, k = a.shape
    n = w.shape[2]
    assert m % tm == 0 and n % tn == 0
    operands = [a]
    in_specs = [pl.BlockSpec((tm, k), lambda i, j: (i, 0))]
    if gain is not None:
        operands.append(gain.reshape(1, k))
        in_specs.append(pl.BlockSpec((1, k), lambda i, j: (0, 0)))
    operands.append(w)
    in_specs.append(pl.BlockSpec((None, k, tn), lambda i, j: (layer, 0, j)))
    if residual is not None:
        operands.append(residual)
        in_specs.append(pl.BlockSpec((tm, tn), lambda i, j: (i, j)))
    pipelined = tm * k * 4 + k * tn * 2 + tm * tn * 4 * (2 if residual is not None else 1)
    return pl.pallas_call(
        functools.partial(_matmul_kernel, norm=gain is not None, residual=residual is not None),
        grid=(m // tm, n // tn),
        in_specs=in_specs,
        out_specs=pl.BlockSpec((tm, tn), lambda i, j: (i, out_block(j))),
        out_shape=jax.ShapeDtypeStruct((m, n), F32),
        scratch_shapes=[pltpu.VMEM((tm, k), BF16)],
        compiler_params=pltpu.CompilerParams(
            dimension_semantics=("parallel", "arbitrary"),
            vmem_limit_bytes=_vmem_limit(pipelined, tm * k * 2 + tm * k * 4)),
        name="matmul",
    )(*operands)


def _s5_kernel(u_ref, d_ref, bm_ref, cm_ref, pw_ref, o_ref, x_ref, carry_ref):
    nb, tc = u_ref.shape[0], u_ref.shape[1]
    ng = tc // SUBLANES

    @pl.when(pl.program_id(1) == 0)
    def _reset_state():
        carry_ref[...] = jnp.zeros_like(carry_ref)

    for b in range(nb):
        bu = jnp.dot(u_ref[b].astype(BF16), bm_ref[0], preferred_element_type=F32)
        re = bu[:, :S5_ST].reshape(ng, SUBLANES, S5_ST)
        im = bu[:, S5_ST:].reshape(ng, SUBLANES, S5_ST)
        for idx, k in enumerate((1, 2, 4)):
            lre = pw_ref[0, 2 * idx]
            lim = pw_ref[0, 2 * idx + 1]
            sre = pltpu.roll(re, k, 1)
            sim = pltpu.roll(im, k, 1)
            re, im = re + (lre * sre - lim * sim), im + (lre * sim + lim * sre)
        x_ref[b, :, :S5_ST] = re.reshape(tc, S5_ST)
        x_ref[b, :, S5_ST:] = im.reshape(tc, S5_ST)

    cyre = pw_ref[0, 6]
    cyim = pw_ref[0, 7]

    def _carry_group(g, carry):
        r0 = pl.multiple_of(g * SUBLANES, SUBLANES)
        last = SUBLANES - 1
        out = []
        for b in range(nb):
            cr, ci = carry[2 * b], carry[2 * b + 1]
            xr = x_ref[b, pl.ds(r0, SUBLANES), :S5_ST] + (cyre * cr - cyim * ci)
            xi = x_ref[b, pl.ds(r0, SUBLANES), S5_ST:] + (cyre * ci + cyim * cr)
            x_ref[b, pl.ds(r0, SUBLANES), :S5_ST] = xr
            x_ref[b, pl.ds(r0, SUBLANES), S5_ST:] = xi
            out += [jnp.broadcast_to(xr[last:, :], (SUBLANES, S5_ST)),
                    jnp.broadcast_to(xi[last:, :], (SUBLANES, S5_ST))]
        return tuple(out)

    carry = lax.fori_loop(0, ng, _carry_group, tuple(carry_ref[k] for k in range(2 * nb)))
    for k in range(2 * nb):
        carry_ref[k] = carry[k]

    for b in range(nb):
        y = jnp.dot(x_ref[b].astype(BF16), cm_ref[0], preferred_element_type=F32)
        o_ref[b] = y + d_ref[...] * u_ref[b]


def _s5_discretise(a_re, a_im, b_re, b_im, c_re, c_im, log_dt):
    dt = jnp.exp(log_dt)[:, None]

    def a_pow(k):
        mag = jnp.exp(a_re * dt * k)
        return mag * jnp.cos(a_im * dt * k), mag * jnp.sin(a_im * dt * k)

    abar_re, abar_im = a_pow(1.0)
    num_re, num_im = abar_re - 1.0, abar_im
    den = a_re * a_re + a_im * a_im
    coef_re = (num_re * a_re + num_im * a_im) / den
    coef_im = (num_im * a_re - num_re * a_im) / den
    bbar_re = coef_re[..., None] * b_re - coef_im[..., None] * b_im
    bbar_im = coef_re[..., None] * b_im + coef_im[..., None] * b_re

    eye = jnp.eye(S5_GB, dtype=F32)

    def in_block(b):
        b = b.reshape(S5_NGB, S5_GB, SSM_STATE, SSM_GROUP)
        return jnp.einsum('bgph,gk->bghkp', b, eye).reshape(S5_NGB, S5_IN, S5_ST)

    def out_block(c):
        c = c.reshape(S5_NGB, S5_GB, SSM_GROUP, SSM_STATE)
        return jnp.einsum('bghp,gk->bgpkh', c, eye).reshape(S5_NGB, S5_ST, S5_IN)

    bmat = jnp.concatenate([in_block(bbar_re), in_block(bbar_im)], axis=2).astype(BF16)
    cmat = jnp.concatenate([out_block(c_re), out_block(-c_im)], axis=1).astype(BF16)

    rows = jnp.arange(SUBLANES)[:, None, None]

    def tile(vals):
        return vals.reshape(SUBLANES, S5_NGB, S5_ST).transpose(1, 0, 2)

    tiles = []
    for k in (1, 2, 4):
        pr, pi = a_pow(float(k))
        tiles.append(tile(jnp.where(rows >= k, pr[None], 0.0)))
        tiles.append(tile(jnp.where(rows >= k, pi[None], 0.0)))
    steps = (rows + 1).astype(F32)
    mag = jnp.exp(a_re[None] * dt[None] * steps)
    tiles.append(tile(mag * jnp.cos(a_im[None] * dt[None] * steps)))
    tiles.append(tile(mag * jnp.sin(a_im[None] * dt[None] * steps)))
    pw = jnp.stack(tiles, axis=1)
    return bmat, cmat, pw


def _s5_scan(proj, ssm_d, bmat, cmat, pw, *, batch, seq):
    nc = seq // S5_CHUNK
    proj = proj.reshape(batch, seq, proj.shape[1])
    pipelined = batch * S5_CHUNK * S5_IN * 4 * 2 + S5_IN * 2 * S5_ST * 2 * 2 + 8 * SUBLANES * S5_ST * 4
    resident = batch * S5_CHUNK * 2 * S5_ST * 4 * 4
    out = pl.pallas_call(
        _s5_kernel,
        grid=(S5_NGB, nc),
        in_specs=[
            pl.BlockSpec((batch, S5_CHUNK, S5_IN), lambda g, c: (0, c, COL_SSM // S5_IN + g)),
            pl.BlockSpec((1, S5_IN), lambda g, c: (0, g)),
            pl.BlockSpec((1, S5_IN, 2 * S5_ST), lambda g, c: (g, 0, 0)),
            pl.BlockSpec((1, 2 * S5_ST, S5_IN), lambda g, c: (g, 0, 0)),
            pl.BlockSpec((1, 8, SUBLANES, S5_ST), lambda g, c: (g, 0, 0, 0)),
        ],
        out_specs=pl.BlockSpec((batch, S5_CHUNK, S5_IN), lambda g, c: (0, c, g)),
        out_shape=jax.ShapeDtypeStruct((batch, seq, SSM_WIDTH), F32),
        scratch_shapes=[pltpu.VMEM((batch, S5_CHUNK, 2 * S5_ST), F32),
                        pltpu.VMEM((2 * batch, SUBLANES, S5_ST), F32)],
        compiler_params=pltpu.CompilerParams(
            dimension_semantics=("parallel", "arbitrary"),
            vmem_limit_bytes=_vmem_limit(pipelined, resident)),
        name="s5_scan",
    )(proj, ssm_d.reshape(1, SSM_WIDTH), bmat, cmat, pw)
    return out.reshape(batch * seq, SSM_WIDTH)


def _s5_glu_kernel(y_ref, w_ref, b_ref, o_ref):
    y = _gelu(y_ref[...])
    z = jnp.dot(y.astype(BF16), w_ref[...], preferred_element_type=F32) + b_ref[...]
    o_ref[...] = (y * jax.nn.sigmoid(z)).astype(o_ref.dtype)


def _s5_glu(y, w_glu, layer, b_glu, *, tm):
    n = y.shape[0]
    return pl.pallas_call(
        _s5_glu_kernel,
        grid=(n // tm,),
        in_specs=[pl.BlockSpec((tm, SSM_WIDTH), lambda i: (i, 0)),
                  pl.BlockSpec((None, SSM_WIDTH, SSM_WIDTH), lambda i: (layer, 0, 0)),
                  pl.BlockSpec((1, SSM_WIDTH), lambda i: (0, 0))],
        out_specs=pl.BlockSpec((tm, SSM_WIDTH), lambda i: (i, 0)),
        out_shape=jax.ShapeDtypeStruct((n, SSM_WIDTH), BF16),
        compiler_params=pltpu.CompilerParams(
            dimension_semantics=("parallel",),
            vmem_limit_bytes=_vmem_limit(2 * tm * SSM_WIDTH * 4 + SSM_WIDTH * SSM_WIDTH * 2,
                                         4 * tm * SSM_WIDTH * 4)),
        name="s5_glu",
    )(y, w_glu, b_glu.reshape(1, SSM_WIDTH))


def _swa_kernel(sink_ref, q_ref, kp_ref, kc_ref, vp_ref, vc_ref, qg_ref, kg_ref, o_ref, *, blocks_per_seq):
    first = (pl.program_id(0) % blocks_per_seq) == 0
    q = q_ref[...]
    k = jnp.concatenate([kp_ref[...], kc_ref[...]], axis=0)
    v = jnp.concatenate([vp_ref[...], vc_ref[...]], axis=0)
    rows = Q_PER_KV * WINDOW
    qpos = (lax.broadcasted_iota(jnp.int32, (rows, 2 * WINDOW), 0) & (WINDOW - 1)) + WINDOW
    kpos = lax.broadcasted_iota(jnp.int32, (rows, 2 * WINDOW), 1)
    diff = qpos - kpos
    first_key = jnp.where(first, WINDOW, 0)
    valid = (diff >= 0) & (diff < WINDOW) & (kpos >= first_key)
    outs = []
    for kv in range(N_KV_HEADS):
        ks = slice(kv * HEAD_DIM, (kv + 1) * HEAD_DIM)
        k_h = _rms(k[:, ks], kg_ref[...]).astype(BF16)
        v_h = v[:, ks].astype(BF16)
        heads = range(kv * Q_PER_KV, (kv + 1) * Q_PER_KV)
        q_grp = jnp.concatenate([q[:, h * HEAD_DIM:(h + 1) * HEAD_DIM] for h in heads], axis=0)
        q_grp = _rms(q_grp, qg_ref[...]).astype(BF16)
        sink = jnp.concatenate([jnp.full((WINDOW, 1), sink_ref[h], F32) for h in heads], axis=0)
        s = lax.dot_general(q_grp, k_h, (((1,), (1,)), ((), ())), preferred_element_type=F32)
        s = jnp.where(valid, s * (HEAD_DIM ** -0.5), -1e30)
        m = jnp.maximum(jnp.max(s, axis=-1, keepdims=True), sink)
        p = jnp.exp(s - m)
        denom = jnp.sum(p, axis=-1, keepdims=True) + jnp.exp(sink - m)
        o_grp = jnp.dot(p.astype(BF16), v_h, preferred_element_type=F32) / denom
        outs += [o_grp[g * WINDOW:(g + 1) * WINDOW] for g in range(Q_PER_KV)]
    o_ref[...] = jnp.concatenate(outs, axis=-1).astype(o_ref.dtype)


def _swa(proj, q_g, k_g, sinks, *, batch, seq):
    n = batch * seq
    nb = seq // WINDOW
    prev = lambda r: jnp.maximum(r - 1, 0)
    pipelined = WINDOW * (2 * ATTN_WIDTH + 4 * KV_WIDTH) * 4
    return pl.pallas_call(
        functools.partial(_swa_kernel, blocks_per_seq=nb),
        grid=(n // WINDOW,),
        in_specs=[
            pl.BlockSpec(memory_space=pltpu.SMEM),
            pl.BlockSpec((WINDOW, ATTN_WIDTH), lambda r: (r, COL_Q // ATTN_WIDTH)),
            pl.BlockSpec((WINDOW, KV_WIDTH), lambda r: (prev(r), COL_K // KV_WIDTH)),
            pl.BlockSpec((WINDOW, KV_WIDTH), lambda r: (r, COL_K // KV_WIDTH)),
            pl.BlockSpec((WINDOW, KV_WIDTH), lambda r: (prev(r), COL_V // KV_WIDTH)),
            pl.BlockSpec((WINDOW, KV_WIDTH), lambda r: (r, COL_V // KV_WIDTH)),
            pl.BlockSpec((1, HEAD_DIM), lambda r: (0, 0)),
            pl.BlockSpec((1, HEAD_DIM), lambda r: (0, 0)),
        ],
        out_specs=pl.BlockSpec((WINDOW, ATTN_WIDTH), lambda r: (r, 0)),
        out_shape=jax.ShapeDtypeStruct((n, ATTN_WIDTH), BF16),
        compiler_params=pltpu.CompilerParams(
            dimension_semantics=("parallel",),
            vmem_limit_bytes=_vmem_limit(pipelined, 16 << 20)),
        name="swa",
    )(sinks, proj, proj, proj, proj, proj, q_g.reshape(1, HEAD_DIM), k_g.reshape(1, HEAD_DIM))


def _sgu_kernel(z_ref, g_ref, b_ref, w_ref, bs_ref, o_ref):
    tm = z_ref.shape[0]
    z = _gelu(z_ref[...])
    u = z[:, :SGU_WIDTH]
    v = z[:, SGU_WIDTH:]
    vc = v - jnp.mean(v, axis=-1, keepdims=True)
    vn = vc * lax.rsqrt(jnp.mean(vc * vc, axis=-1, keepdims=True) + EPS) * g_ref[...] + b_ref[...]
    vn = vn.astype(BF16)
    row = lax.broadcasted_iota(jnp.int32, (SGU_CHUNK, SGU_CHUNK), 0)
    col = lax.broadcasted_iota(jnp.int32, (SGU_CHUNK, SGU_CHUNK), 1)
    for h in range(SGU_HEADS):
        w_h = jnp.where(row >= col, w_ref[h], 0.0).astype(BF16)
        cs = slice(h * SGU_HEAD_DIM, (h + 1) * SGU_HEAD_DIM)
        for c in range(tm // SGU_CHUNK):
            rs = slice(c * SGU_CHUNK, (c + 1) * SGU_CHUNK)
            mixed = jnp.dot(w_h, vn[rs, cs], preferred_element_type=F32) + bs_ref[:, h:h + 1]
            o_ref[rs, cs] = (u[rs, cs] * mixed).astype(o_ref.dtype)


def _sgu(proj, ln_g, ln_b, w_s, b_s, *, tm):
    n = proj.shape[0]
    return pl.pallas_call(
        _sgu_kernel,
        grid=(n // tm,),
        in_specs=[
            pl.BlockSpec((tm, 2 * SGU_WIDTH), lambda i: (i, COL_SGU // (2 * SGU_WIDTH))),
            pl.BlockSpec((1, SGU_WIDTH), lambda i: (0, 0)),
            pl.BlockSpec((1, SGU_WIDTH), lambda i: (0, 0)),
            pl.BlockSpec((SGU_HEADS, SGU_CHUNK, SGU_CHUNK), lambda i: (0, 0, 0)),
            pl.BlockSpec((SGU_CHUNK, SGU_HEADS), lambda i: (0, 0)),
        ],
        out_specs=pl.BlockSpec((tm, SGU_WIDTH), lambda i: (i, 0)),
        out_shape=jax.ShapeDtypeStruct((n, SGU_WIDTH), BF16),
        compiler_params=pltpu.CompilerParams(
            dimension_semantics=("parallel",),
            vmem_limit_bytes=_vmem_limit(tm * 3 * SGU_WIDTH * 4, 6 * tm * 2 * SGU_WIDTH * 4)),
        name="sgu",
    )(proj, ln_g.reshape(1, SGU_WIDTH), ln_b.reshape(1, SGU_WIDTH), w_s, b_s.T)


def _merge_kernel(ya_ref, yb_ref, yc_ref, w_ref, ga_ref, gb_ref, gc_ref, o_ref):
    acc = None
    for n, (y_ref, gate_ref) in enumerate(((ya_ref, ga_ref), (yb_ref, gb_ref), (yc_ref, gc_ref))):
        branch = jnp.dot(y_ref[...], w_ref[n], preferred_element_type=F32)
        term = jax.nn.sigmoid(gate_ref[...]) * branch
        acc = term if acc is None else acc + term
    o_ref[...] = acc


def _merge(ya, yb, yc, w_branch, layer, proj, *, tm, tn):
    n = ya.shape[0]
    y_spec = pl.BlockSpec((tm, BRANCH_WIDTH), lambda i, j: (i, 0))

    def gate_spec(b):
        return pl.BlockSpec((tm, tn), lambda i, j: (i, (COL_GATE + b * D_MODEL) // tn + j))

    pipelined = 3 * tm * BRANCH_WIDTH * 2 + 3 * BRANCH_WIDTH * tn * 2 + 4 * tm * tn * 4
    return pl.pallas_call(
        _merge_kernel,
        grid=(n // tm, D_MODEL // tn),
        in_specs=[y_spec, y_spec, y_spec,
                  pl.BlockSpec((None, N_BRANCH, BRANCH_WIDTH, tn), lambda i, j: (layer, 0, 0, j)),
                  gate_spec(0), gate_spec(1), gate_spec(2)],
        out_specs=pl.BlockSpec((tm, tn), lambda i, j: (i, j)),
        out_shape=jax.ShapeDtypeStruct((n, D_MODEL), F32),
        compiler_params=pltpu.CompilerParams(
            dimension_semantics=("parallel", "parallel"),
            vmem_limit_bytes=_vmem_limit(pipelined, 4 * tm * tn * 4)),
        name="merge",
    )(ya, yb, yc, w_branch, proj, proj, proj)


def _top_values(s, count):
    rows = []
    rank = jnp.full(s.shape, float(count), F32)
    for r in range(count):
        m = jnp.max(s, axis=0, keepdims=True)
        rows.append(m)
        hit = s == m
        rank = jnp.where(hit, float(r), rank)
        s = jnp.where(hit, -jnp.inf, s)
    return jnp.concatenate(rows, axis=0), rank


def _peer_select_kernel(q_ref, keys_ref, n_ref, w1_ref, r2_ref, w2_ref):
    nt = lax.dot_general
    contract_last = (((1,), (1,)), ((), ()))
    for h in range(PEER_HEADS):
        c0 = h * PEER_QDIM
        q1 = q_ref[:, c0:c0 + PEER_HALF].astype(BF16)
        q2 = q_ref[:, c0 + PEER_HALF:c0 + PEER_QDIM].astype(BF16)
        s1 = nt(keys_ref[0, h], q1, contract_last, preferred_element_type=F32)
        s2 = nt(keys_ref[1, h], q2, contract_last, preferred_element_type=F32)
        a, _ = _top_values(s1, PEER_TOPK)
        b, rank2 = _top_values(s2, PEER_TOPK)
        cand = [a[0:1] + b]
        cand += [a[p:p + 1] + b[0:SUBLANES] for p in range(1, SUBLANES)]
        cand.append(a[SUBLANES:] + b[0:1])
        cand = jnp.concatenate(cand, axis=0)
        work = cand
        for _ in range(PEER_TOPK - 1):
            m = jnp.max(work, axis=0, keepdims=True)
            work = jnp.where(work == m, -jnp.inf, work)
        thr = jnp.max(work, axis=0, keepdims=True)
        top = a[0:1] + b[0:1]
        z = jnp.sum(jnp.where(cand >= thr, jnp.exp(cand - top), 0.0), axis=0, keepdims=True)
        count = jnp.zeros(s1.shape, F32)
        for r in range(PEER_TOPK):
            count = count + jnp.where(s1 + b[r:r + 1] >= thr, 1.0, 0.0)
        n_ref[h] = count[:, None, :]
        w1_ref[h] = jnp.exp(s1 - a[0:1])[:, None, :]
        packed = (N_KEYS // BF16_ROWS, BF16_ROWS, s2.shape[1])
        r2_ref[h] = rank2.astype(BF16).reshape(packed)
        w2_ref[h] = (jnp.exp(s2 - b[0:1]) / z).astype(BF16).reshape(packed)


def _peer_select(q, keys, layer, *, tm):
    n = q.shape[0]
    rows = jax.ShapeDtypeStruct((PEER_HEADS, N_KEYS, 1, n), F32)
    rows_spec = pl.BlockSpec((PEER_HEADS, N_KEYS, 1, tm), lambda i: (0, 0, 0, i))
    big = jax.ShapeDtypeStruct((PEER_HEADS, N_KEYS // BF16_ROWS, BF16_ROWS, n), BF16)
    big_spec = pl.BlockSpec((PEER_HEADS, N_KEYS // BF16_ROWS, BF16_ROWS, tm), lambda i: (0, 0, 0, i))
    return pl.pallas_call(
        _peer_select_kernel,
        grid=(n // tm,),
        in_specs=[pl.BlockSpec((tm, PEER_HEADS * PEER_QDIM), lambda i: (i, 0)),
                  pl.BlockSpec((None, 2, PEER_HEADS, N_KEYS, PEER_HALF), lambda i: (layer, 0, 0, 0, 0))],
        out_specs=[rows_spec, rows_spec, big_spec, big_spec],
        out_shape=[rows, rows, big, big],
        compiler_params=pltpu.CompilerParams(
            dimension_semantics=("parallel",),
            vmem_limit_bytes=_vmem_limit(tm * PEER_HEADS * PEER_QDIM * 4 + 4 * PEER_HEADS * N_KEYS * tm * 4,
                                         16 << 20)),
        name="peer_select",
    )(q, keys)


def _peer_stage(u_ref, vt_ref, n_ref, w1_ref, r2_ref, w2_ref, xn_ref, acc_ref, h_new, h_old):
    te, tm = u_ref.shape[0], xn_ref.shape[0]
    packed = (N_KEYS // BF16_ROWS, BF16_ROWS, LANES)

    def row(ref, hd, i, ls):
        return jnp.broadcast_to(ref[hd, i, :, ls], (BF16_ROWS, LANES)).astype(BF16)

    def gate_block(i, c):
        rs = slice(i * N_KEYS, (i + 1) * N_KEYS)
        ls = slice(c * LANES, (c + 1) * LANES)
        gate = None
        for hd in range(PEER_HEADS):
            margin = jnp.maximum(row(n_ref, hd, i, ls) - r2_ref[hd, :, :, ls], 0.0)
            term = jnp.minimum(margin, w2_ref[hd, :, :, ls]) * row(w1_ref, hd, i, ls)
            gate = term if gate is None else gate + term
        act = _gelu(h_old[rs, ls]).reshape(packed).astype(BF16) * gate
        return act.reshape(N_KEYS, LANES)

    def output_piece(t):
        cols = [jnp.concatenate([gate_block(i, c) for i in range(te // N_KEYS)], axis=0)
                for c in range(t // LANES, (t + MXU_TILE) // LANES)]
        a = jnp.concatenate(cols, axis=1)
        acc_ref[:, t:t + MXU_TILE] += jnp.dot(vt_ref[...], a, preferred_element_type=F32)

    def hidden_piece(r):
        rs = slice(r, r + MXU_TILE)
        h_new[rs, :] = lax.dot_general(u_ref[rs, :], xn_ref[...], (((1,), (1,)), ((), ())),
                                       preferred_element_type=F32)

    hidden_per_output = (te // MXU_TILE) // (tm // MXU_TILE)
    assert hidden_per_output * tm == te
    for k in range(tm // MXU_TILE):
        if h_new is not None:
            for r in range(hidden_per_output):
                hidden_piece((k * hidden_per_output + r) * MXU_TILE)
        if h_old is not None:
            output_piece(k * MXU_TILE)


def _peer_ffn_kernel(x_ref, g_ref, u_ref, vt_ref, n_ref, w1_ref, r2_ref, w2_ref, o_ref,
                     xn_ref, acc_ref, h0_ref, h1_ref, *, tiles):
    s = pl.program_id(1)
    stage = functools.partial(_peer_stage, u_ref, vt_ref, n_ref, w1_ref, r2_ref, w2_ref, xn_ref, acc_ref)
    middle = (s > 0) & (s < tiles)

    @pl.when(s == 0)
    def _first_step():
        xn_ref[...] = _rms(x_ref[...], g_ref[...]).astype(BF16)
        acc_ref[...] = jnp.zeros_like(acc_ref)
        stage(h0_ref, None)

    @pl.when(middle & (s % 2 == 0))
    def _even_step():
        stage(h0_ref, h1_ref)

    @pl.when(middle & (s % 2 == 1))
    def _odd_step():
        stage(h1_ref, h0_ref)

    @pl.when(s == tiles)
    def _last_step():
        stage(None, h1_ref if tiles % 2 == 0 else h0_ref)
        o_ref[...] = x_ref[...] + acc_ref[...].T


PEER_PIPELINE_LAG = 1


def _peer_ffn(x, gain, u_bf, vt_bf, layer, count, w1, rank2, w2, *, tm, te):
    n = x.shape[0]
    slabs = te // N_KEYS
    tiles = N_EXPERTS // te
    clip = lambda s, lag: jnp.clip(s - lag, 0, tiles - 1)
    row_spec = pl.BlockSpec((PEER_HEADS, slabs, 1, tm), lambda t, s: (0, clip(s, PEER_PIPELINE_LAG), 0, t))
    full_spec = pl.BlockSpec((PEER_HEADS, N_KEYS // BF16_ROWS, BF16_ROWS, tm), lambda t, s: (0, 0, 0, t))
    pipelined = (2 * tm * D_MODEL * 4 + 2 * te * D_MODEL * 2
                 + 2 * PEER_HEADS * (slabs * SUBLANES * 4 + N_KEYS * 2) * tm)
    resident = tm * D_MODEL * 2 + D_MODEL * tm * 4 + 2 * te * tm * 4 + D_MODEL * tm * 4
    return pl.pallas_call(
        functools.partial(_peer_ffn_kernel, tiles=tiles),
        grid=(n // tm, tiles + PEER_PIPELINE_LAG),
        in_specs=[
            pl.BlockSpec((tm, D_MODEL), lambda t, s: (t, 0)),
            pl.BlockSpec((1, D_MODEL), lambda t, s: (0, 0)),
            pl.BlockSpec((None, te, D_MODEL), lambda t, s: (layer, clip(s, 0), 0)),
            pl.BlockSpec((None, D_MODEL, te), lambda t, s: (layer, 0, clip(s, PEER_PIPELINE_LAG))),
            row_spec, row_spec, full_spec, full_spec,
        ],
        out_specs=pl.BlockSpec((tm, D_MODEL), lambda t, s: (t, 0)),
        out_shape=jax.ShapeDtypeStruct((n, D_MODEL), F32),
        scratch_shapes=[pltpu.VMEM((tm, D_MODEL), BF16),
                        pltpu.VMEM((D_MODEL, tm), F32),
                        pltpu.VMEM((te, tm), F32), pltpu.VMEM((te, tm), F32)],
        compiler_params=pltpu.CompilerParams(
            dimension_semantics=("parallel", "arbitrary"),
            vmem_limit_bytes=_vmem_limit(pipelined, resident)),
        name="peer_ffn",
    )(x, gain.reshape(1, D_MODEL), u_bf, vt_bf, count, w1, rank2, w2)


IN_PROJ_TN = 2 * KV_WIDTH


def _in_proj_out_block(j):
    kv = (SSM_WIDTH + ATTN_WIDTH) // IN_PROJ_TN
    return jnp.where(j < kv, j, jnp.where(j == kv, COL_K // IN_PROJ_TN, j - 1))


def kernel(x, norm1_g, w_in, ssm_a_re, ssm_a_im, ssm_b_re, ssm_b_im, ssm_c_re, ssm_c_im, ssm_d, ssm_log_dt,
           w_glu, b_glu, q_norm_g, k_norm_g, attn_sinks, sgu_ln_g, sgu_ln_b, sgu_w, sgu_b, w_branch, w_out,
           norm2_g, w_query, peer_keys, peer_u, peer_v):
    batch, seq, d_model = x.shape
    assert d_model == D_MODEL and seq % S5_CHUNK == 0
    xt = x.reshape(batch * seq, D_MODEL)
    w_in, w_glu, w_branch, w_out, w_query, peer_keys, peer_u = (
        w.astype(BF16) for w in (w_in, w_glu, w_branch, w_out, w_query, peer_keys, peer_u))
    peer_vt = jnp.swapaxes(peer_v, 1, 2).astype(BF16)
    for l in range(DEPTH):
        proj = _matmul(xt, w_in, l, gain=norm1_g[l], out_block=_in_proj_out_block, tm=1024, tn=IN_PROJ_TN)
        bmat, cmat, pw = _s5_discretise(ssm_a_re[l], ssm_a_im[l], ssm_b_re[l], ssm_b_im[l],
                                        ssm_c_re[l], ssm_c_im[l], ssm_log_dt[l])
        y_ssm = _s5_scan(proj, ssm_d[l], bmat, cmat, pw, batch=batch, seq=seq)
        y_a = _s5_glu(y_ssm, w_glu, l, b_glu[l], tm=512)
        y_b = _swa(proj, q_norm_g[l], k_norm_g[l], attn_sinks[l], batch=batch, seq=seq)
        y_c = _sgu(proj, sgu_ln_g[l], sgu_ln_b[l], sgu_w[l], sgu_b[l], tm=256)
        merged = _merge(y_a, y_b, y_c, w_branch, l, proj, tm=1024, tn=512)
        xt = _matmul(merged, w_out, l, residual=xt, tm=512, tn=D_MODEL)
        q = _matmul(xt, w_query, l, gain=norm2_g[l], tm=512, tn=D_MODEL)
        count, w1, rank2, w2 = _peer_select(q, peer_keys, l, tm=256)
        xt = _peer_ffn(xt, norm2_g[l], peer_u, peer_vt, l, count, w1, rank2, w2, tm=512, te=1024)
    return xt.reshape(batch, seq, D_MODEL)
```

```python
import functools
import math

import jax
import jax.numpy as jnp
from jax import lax
from jax.experimental import pallas as pl
from jax.experimental.pallas import tpu as pltpu

F32 = jnp.float32
BF16 = jnp.bfloat16

D_MODEL = 2048
DEPTH = 2
SSM_WIDTH = 1024
SSM_GROUP = 16
SSM_GROUPS = SSM_WIDTH // SSM_GROUP
SSM_STATE = 64
HEAD_DIM = 64
N_Q_HEADS = 16
N_KV_HEADS = 4
Q_PER_KV = N_Q_HEADS // N_KV_HEADS
WINDOW = 128
ATTN_WIDTH = N_Q_HEADS * HEAD_DIM
KV_WIDTH = N_KV_HEADS * HEAD_DIM
SGU_WIDTH = 1024
SGU_HEADS = 8
SGU_HEAD_DIM = SGU_WIDTH // SGU_HEADS
SGU_CHUNK = 128
N_BRANCH = 3
BRANCH_WIDTH = 1024
PEER_HEADS = 8
PEER_QDIM = 256
PEER_HALF = PEER_QDIM // 2
N_KEYS = 128
N_EXPERTS = N_KEYS * N_KEYS
PEER_TOPK = 16
EPS = 1e-6

V7X_VMEM_BYTES = 64 * 1024 * 1024
SUBLANES = 8
LANES = 128
BF16_ROWS = 2 * SUBLANES
MXU_TILE = 256

COL_SSM = 0
COL_Q = COL_SSM + SSM_WIDTH
COL_K = COL_Q + ATTN_WIDTH
COL_V = COL_K + KV_WIDTH
COL_SGU = COL_V + KV_WIDTH
COL_GATE = COL_SGU + 2 * SGU_WIDTH
IN_COLS = COL_GATE + N_BRANCH * D_MODEL

S5_GB = 16
S5_NGB = SSM_GROUPS // S5_GB
S5_IN = S5_GB * SSM_GROUP
S5_ST = S5_GB * SSM_STATE
S5_CHUNK = 512

SQRT_HALF = math.sqrt(0.5)


def _vmem_limit(pipelined_bytes, resident_bytes):
    need = 2 * pipelined_bytes + resident_bytes + (4 << 20)
    return int(min(need, V7X_VMEM_BYTES - (8 << 20)))


def _gelu(x):
    return 0.5 * x * (1.0 + lax.erf(x * SQRT_HALF))


def _rms(x, g):
    return x * lax.rsqrt(jnp.mean(x * x, axis=-1, keepdims=True) + EPS) * g


def _matmul_kernel(*refs, norm, residual):
    refs = list(refs)
    a_ref = refs.pop(0)
    g_ref = refs.pop(0) if norm else None
    w_ref = refs.pop(0)
    r_ref = refs.pop(0) if residual else None
    o_ref, abf_ref = refs

    @pl.when(pl.program_id(1) == 0)
    def _prepare_lhs():
        a = a_ref[...]
        if norm:
            a = _rms(a, g_ref[...])
        abf_ref[...] = a.astype(BF16)

    acc = jnp.dot(abf_ref[...], w_ref[...], preferred_element_type=F32)
    if residual:
        acc = r_ref[...] + acc
    o_ref[...] = acc


def _matmul(a, w, layer, *, gain=None, residual=None, tm, tn):
    m, k = a.shape
    n = w.shape[2]
    assert m % tm == 0 and n % tn == 0
    operands = [a]
    in_specs = [pl.BlockSpec((tm, k), lambda i, j: (i, 0))]
    if gain is not None:
        operands.append(gain.reshape(1, k))
        in_specs.append(pl.BlockSpec((1, k), lambda i, j: (0, 0)))
    operands.append(w)
    in_specs.append(pl.BlockSpec((None, k, tn), lambda i, j: (layer, 0, j)))
    if residual is not None:
        operands.append(residual)
        in_specs.append(pl.BlockSpec((tm, tn), lambda i, j: (i, j)))
    pipelined = tm * k * 4 + k * tn * 2 + tm * tn * 4 * (2 if residual is not None else 1)
    return pl.pallas_call(
        functools.partial(_matmul_kernel, norm=gain is not None, residual=residual is not None),
        grid=(m // tm, n // tn),
        in_specs=in_specs,
        out_specs=pl.BlockSpec((tm, tn), lambda i, j: (i, j)),
        out_shape=jax.ShapeDtypeStruct((m, n), F32),
        scratch_shapes=[pltpu.VMEM((tm, k), BF16)],
        compiler_params=pltpu.CompilerParams(
            dimension_semantics=("parallel", "arbitrary"),
            vmem_limit_bytes=_vmem_limit(pipelined, tm * k * 2 + tm * k * 4)),
        name="matmul",
    )(*operands)


def _s5_kernel(u_ref, d_ref, bm_ref, cm_ref, pw_ref, o_ref, x_ref, carry_ref):
    nb, tc = u_ref.shape[0], u_ref.shape[1]
    ng = tc // SUBLANES

    @pl.when(pl.program_id(1) == 0)
    def _reset_state():
        carry_ref[...] = jnp.zeros_like(carry_ref)

    for b in range(nb):
        bu = jnp.dot(u_ref[b].astype(BF16), bm_ref[0], preferred_element_type=F32)
        re = bu[:, :S5_ST].reshape(ng, SUBLANES, S5_ST)
        im = bu[:, S5_ST:].reshape(ng, SUBLANES, S5_ST)
        for idx, k in enumerate((1, 2, 4)):
            lre = pw_ref[0, 2 * idx]
            lim = pw_ref[0, 2 * idx + 1]
            sre = pltpu.roll(re, k, 1)
            sim = pltpu.roll(im, k, 1)
            re, im = re + (lre * sre - lim * sim), im + (lre * sim + lim * sre)
        x_ref[b, :, :S5_ST] = re.reshape(tc, S5_ST)
        x_ref[b, :, S5_ST:] = im.reshape(tc, S5_ST)

    cyre = pw_ref[0, 6]
    cyim = pw_ref[0, 7]

    def _carry_group(g, carry):
        r0 = pl.multiple_of(g * SUBLANES, SUBLANES)
        last = SUBLANES - 1
        out = []
        for b in range(nb):
            cr, ci = carry[2 * b], carry[2 * b + 1]
            xr = x_ref[b, pl.ds(r0, SUBLANES), :S5_ST] + (cyre * cr - cyim * ci)
            xi = x_ref[b, pl.ds(r0, SUBLANES), S5_ST:] + (cyre * ci + cyim * cr)
            x_ref[b, pl.ds(r0, SUBLANES), :S5_ST] = xr
            x_ref[b, pl.ds(r0, SUBLANES), S5_ST:] = xi
            out += [jnp.broadcast_to(xr[last:, :], (SUBLANES, S5_ST)),
                    jnp.broadcast_to(xi[last:, :], (SUBLANES, S5_ST))]
        return tuple(out)

    carry = lax.fori_loop(0, ng, _carry_group, tuple(carry_ref[k] for k in range(2 * nb)))
    for k in range(2 * nb):
        carry_ref[k] = carry[k]

    for b in range(nb):
        y = jnp.dot(x_ref[b].astype(BF16), cm_ref[0], preferred_element_type=F32)
        o_ref[b] = y + d_ref[...] * u_ref[b]


def _s5_discretise(a_re, a_im, b_re, b_im, c_re, c_im, log_dt):
    dt = jnp.exp(log_dt)[:, None]

    def a_pow(k):
        mag = jnp.exp(a_re * dt * k)
        return mag * jnp.cos(a_im * dt * k), mag * jnp.sin(a_im * dt * k)

    abar_re, abar_im = a_pow(1.0)
    num_re, num_im = abar_re - 1.0, abar_im
    den = a_re * a_re + a_im * a_im
    coef_re = (num_re * a_re + num_im * a_im) / den
    coef_im = (num_im * a_re - num_re * a_im) / den
    bbar_re = coef_re[..., None] * b_re - coef_im[..., None] * b_im
    bbar_im = coef_re[..., None] * b_im + coef_im[..., None] * b_re

    eye = jnp.eye(S5_GB, dtype=F32)

    def in_block(b):
        b = b.reshape(S5_NGB, S5_GB, SSM_STATE, SSM_GROUP)
        return jnp.einsum('bgph,gk->bghkp', b, eye).reshape(S5_NGB, S5_IN, S5_ST)

    def out_block(c):
        c = c.reshape(S5_NGB, S5_GB, SSM_GROUP, SSM_STATE)
        return jnp.einsum('bghp,gk->bgpkh', c, eye).reshape(S5_NGB, S5_ST, S5_IN)

    bmat = jnp.concatenate([in_block(bbar_re), in_block(bbar_im)], axis=2).astype(BF16)
    cmat = jnp.concatenate([out_block(c_re), out_block(-c_im)], axis=1).astype(BF16)

    rows = jnp.arange(SUBLANES)[:, None, None]

    def tile(vals):
        return vals.reshape(SUBLANES, S5_NGB, S5_ST).transpose(1, 0, 2)

    tiles = []
    for k in (1, 2, 4):
        pr, pi = a_pow(float(k))
        tiles.append(tile(jnp.where(rows >= k, pr[None], 0.0)))
        tiles.append(tile(jnp.where(rows >= k, pi[None], 0.0)))
    steps = (rows + 1).astype(F32)
    mag = jnp.exp(a_re[None] * dt[None] * steps)
    tiles.append(tile(mag * jnp.cos(a_im[None] * dt[None] * steps)))
    tiles.append(tile(mag * jnp.sin(a_im[None] * dt[None] * steps)))
    pw = jnp.stack(tiles, axis=1)
    return bmat, cmat, pw


def _s5_scan(proj, ssm_d, bmat, cmat, pw, *, batch, seq):
    nc = seq // S5_CHUNK
    proj = proj.reshape(batch, seq, proj.shape[1])
    pipelined = batch * S5_CHUNK * S5_IN * 4 * 2 + S5_IN * 2 * S5_ST * 2 * 2 + 8 * SUBLANES * S5_ST * 4
    resident = batch * S5_CHUNK * 2 * S5_ST * 4 * 4
    out = pl.pallas_call(
        _s5_kernel,
        grid=(S5_NGB, nc),
        in_specs=[
            pl.BlockSpec((batch, S5_CHUNK, S5_IN), lambda g, c: (0, c, COL_SSM // S5_IN + g)),
            pl.BlockSpec((1, S5_IN), lambda g, c: (0, g)),
            pl.BlockSpec((1, S5_IN, 2 * S5_ST), lambda g, c: (g, 0, 0)),
            pl.BlockSpec((1, 2 * S5_ST, S5_IN), lambda g, c: (g, 0, 0)),
            pl.BlockSpec((1, 8, SUBLANES, S5_ST), lambda g, c: (g, 0, 0, 0)),
        ],
        out_specs=pl.BlockSpec((batch, S5_CHUNK, S5_IN), lambda g, c: (0, c, g)),
        out_shape=jax.ShapeDtypeStruct((batch, seq, SSM_WIDTH), F32),
        scratch_shapes=[pltpu.VMEM((batch, S5_CHUNK, 2 * S5_ST), F32),
                        pltpu.VMEM((2 * batch, SUBLANES, S5_ST), F32)],
        compiler_params=pltpu.CompilerParams(
            dimension_semantics=("parallel", "arbitrary"),
            vmem_limit_bytes=_vmem_limit(pipelined, resident)),
        name="s5_scan",
    )(proj, ssm_d.reshape(1, SSM_WIDTH), bmat, cmat, pw)
    return out.reshape(batch * seq, SSM_WIDTH)


def _s5_glu_kernel(y_ref, w_ref, b_ref, o_ref):
    y = _gelu(y_ref[...])
    z = jnp.dot(y.astype(BF16), w_ref[...], preferred_element_type=F32) + b_ref[...]
    o_ref[...] = (y * jax.nn.sigmoid(z)).astype(o_ref.dtype)


def _s5_glu(y, w_glu, layer, b_glu, *, tm):
    n = y.shape[0]
    return pl.pallas_call(
        _s5_glu_kernel,
        grid=(n // tm,),
        in_specs=[pl.BlockSpec((tm, SSM_WIDTH), lambda i: (i, 0)),
                  pl.BlockSpec((None, SSM_WIDTH, SSM_WIDTH), lambda i: (layer, 0, 0)),
                  pl.BlockSpec((1, SSM_WIDTH), lambda i: (0, 0))],
        out_specs=pl.BlockSpec((tm, SSM_WIDTH), lambda i: (i, 0)),
        out_shape=jax.ShapeDtypeStruct((n, SSM_WIDTH), BF16),
        compiler_params=pltpu.CompilerParams(
            dimension_semantics=("parallel",),
            vmem_limit_bytes=_vmem_limit(2 * tm * SSM_WIDTH * 4 + SSM_WIDTH * SSM_WIDTH * 2,
                                         4 * tm * SSM_WIDTH * 4)),
        name="s5_glu",
    )(y, w_glu, b_glu.reshape(1, SSM_WIDTH))


def _swa_kernel(sink_ref, q_ref, kp_ref, kc_ref, vp_ref, vc_ref, qg_ref, kg_ref, o_ref, *, blocks_per_seq):
    first = (pl.program_id(0) % blocks_per_seq) == 0
    q = q_ref[...]
    k = jnp.concatenate([kp_ref[...], kc_ref[...]], axis=0)
    v = jnp.concatenate([vp_ref[...], vc_ref[...]], axis=0)
    rows = Q_PER_KV * WINDOW
    qpos = (lax.broadcasted_iota(jnp.int32, (rows, 2 * WINDOW), 0) & (WINDOW - 1)) + WINDOW
    kpos = lax.broadcasted_iota(jnp.int32, (rows, 2 * WINDOW), 1)
    diff = qpos - kpos
    first_key = jnp.where(first, WINDOW, 0)
    valid = (diff >= 0) & (diff < WINDOW) & (kpos >= first_key)
    groups = [range(kv * Q_PER_KV, (kv + 1) * Q_PER_KV) for kv in range(N_KV_HEADS)]
    qb = jnp.stack([jnp.concatenate([q[:, h * HEAD_DIM:(h + 1) * HEAD_DIM] for h in grp], axis=0)
                    for grp in groups])
    kb = jnp.stack([k[:, kv * HEAD_DIM:(kv + 1) * HEAD_DIM] for kv in range(N_KV_HEADS)])
    vb = jnp.stack([v[:, kv * HEAD_DIM:(kv + 1) * HEAD_DIM] for kv in range(N_KV_HEADS)]).astype(BF16)
    qb = _rms(qb, qg_ref[...]).astype(BF16)
    kb = _rms(kb, kg_ref[...]).astype(BF16)
    sink = jnp.stack([jnp.concatenate([jnp.full((WINDOW, 1), sink_ref[h], F32) for h in grp], axis=0)
                      for grp in groups])
    s = jnp.einsum('gqd,gkd->gqk', qb, kb, preferred_element_type=F32)
    s = jnp.where(valid[None], s * (HEAD_DIM ** -0.5), -1e30)
    m = jnp.maximum(jnp.max(s, axis=-1, keepdims=True), sink)
    p = jnp.exp(s - m)
    denom = jnp.sum(p, axis=-1, keepdims=True) + jnp.exp(sink - m)
    o = jnp.einsum('gqk,gkd->gqd', p.astype(BF16), vb, preferred_element_type=F32) / denom
    outs = [o[kv, g * WINDOW:(g + 1) * WINDOW] for kv in range(N_KV_HEADS) for g in range(Q_PER_KV)]
    o_ref[...] = jnp.concatenate(outs, axis=-1).astype(o_ref.dtype)


def _swa(proj, q_g, k_g, sinks, *, batch, seq):
    n = batch * seq
    nb = seq // WINDOW
    prev = lambda r: jnp.maximum(r - 1, 0)
    pipelined = WINDOW * (2 * ATTN_WIDTH + 4 * KV_WIDTH) * 4
    return pl.pallas_call(
        functools.partial(_swa_kernel, blocks_per_seq=nb),
        grid=(n // WINDOW,),
        in_specs=[
            pl.BlockSpec(memory_space=pltpu.SMEM),
            pl.BlockSpec((WINDOW, ATTN_WIDTH), lambda r: (r, COL_Q // ATTN_WIDTH)),
            pl.BlockSpec((WINDOW, KV_WIDTH), lambda r: (prev(r), COL_K // KV_WIDTH)),
            pl.BlockSpec((WINDOW, KV_WIDTH), lambda r: (r, COL_K // KV_WIDTH)),
            pl.BlockSpec((WINDOW, KV_WIDTH), lambda r: (prev(r), COL_V // KV_WIDTH)),
            pl.BlockSpec((WINDOW, KV_WIDTH), lambda r: (r, COL_V // KV_WIDTH)),
            pl.BlockSpec((1, HEAD_DIM), lambda r: (0, 0)),
            pl.BlockSpec((1, HEAD_DIM), lambda r: (0, 0)),
        ],
        out_specs=pl.BlockSpec((WINDOW, ATTN_WIDTH), lambda r: (r, 0)),
        out_shape=jax.ShapeDtypeStruct((n, ATTN_WIDTH), BF16),
        compiler_params=pltpu.CompilerParams(
            dimension_semantics=("parallel",),
            vmem_limit_bytes=_vmem_limit(pipelined, 16 << 20)),
        name="swa",
    )(sinks, proj, proj, proj, proj, proj, q_g.reshape(1, HEAD_DIM), k_g.reshape(1, HEAD_DIM))


def _sgu_kernel(*refs):
    z_refs = refs[:SGU_PARTS]
    g_ref, b_ref, w_ref, bs_ref, o_ref = refs[SGU_PARTS:]
    tm = o_ref.shape[0]
    z = _gelu(jnp.concatenate([r[...] for r in z_refs], axis=1))
    u = z[:, :SGU_WIDTH]
    v = z[:, SGU_WIDTH:]
    vc = v - jnp.mean(v, axis=-1, keepdims=True)
    vn = vc * lax.rsqrt(jnp.mean(vc * vc, axis=-1, keepdims=True) + EPS) * g_ref[...] + b_ref[...]
    vn = vn.astype(BF16)
    row = lax.broadcasted_iota(jnp.int32, (SGU_CHUNK, SGU_CHUNK), 0)
    col = lax.broadcasted_iota(jnp.int32, (SGU_CHUNK, SGU_CHUNK), 1)
    for h in range(SGU_HEADS):
        w_h = jnp.where(row >= col, w_ref[h], 0.0).astype(BF16)
        cs = slice(h * SGU_HEAD_DIM, (h + 1) * SGU_HEAD_DIM)
        for c in range(tm // SGU_CHUNK):
            rs = slice(c * SGU_CHUNK, (c + 1) * SGU_CHUNK)
            mixed = jnp.dot(w_h, vn[rs, cs], preferred_element_type=F32) + bs_ref[:, h:h + 1]
            o_ref[rs, cs] = (u[rs, cs] * mixed).astype(o_ref.dtype)


SGU_PART = math.gcd(COL_SGU, 2 * SGU_WIDTH)
SGU_PARTS = 2 * SGU_WIDTH // SGU_PART


def _sgu(proj, ln_g, ln_b, w_s, b_s, *, tm):
    n = proj.shape[0]
    z_specs = [pl.BlockSpec((tm, SGU_PART), lambda i, p=p: (i, COL_SGU // SGU_PART + p)) for p in range(SGU_PARTS)]
    return pl.pallas_call(
        _sgu_kernel,
        grid=(n // tm,),
        in_specs=z_specs + [
            pl.BlockSpec((1, SGU_WIDTH), lambda i: (0, 0)),
            pl.BlockSpec((1, SGU_WIDTH), lambda i: (0, 0)),
            pl.BlockSpec((SGU_HEADS, SGU_CHUNK, SGU_CHUNK), lambda i: (0, 0, 0)),
            pl.BlockSpec((SGU_CHUNK, SGU_HEADS), lambda i: (0, 0)),
        ],
        out_specs=pl.BlockSpec((tm, SGU_WIDTH), lambda i: (i, 0)),
        out_shape=jax.ShapeDtypeStruct((n, SGU_WIDTH), BF16),
        compiler_params=pltpu.CompilerParams(
            dimension_semantics=("parallel",),
            vmem_limit_bytes=_vmem_limit(tm * 3 * SGU_WIDTH * 4, 6 * tm * 2 * SGU_WIDTH * 4)),
        name="sgu",
    )(*([proj] * SGU_PARTS), ln_g.reshape(1, SGU_WIDTH), ln_b.reshape(1, SGU_WIDTH), w_s, b_s.T)


def _merge_kernel(ya_ref, yb_ref, yc_ref, w_ref, ga_ref, gb_ref, gc_ref, o_ref):
    acc = None
    for n, (y_ref, gate_ref) in enumerate(((ya_ref, ga_ref), (yb_ref, gb_ref), (yc_ref, gc_ref))):
        branch = jnp.dot(y_ref[...], w_ref[n], preferred_element_type=F32)
        term = jax.nn.sigmoid(gate_ref[...]) * branch
        acc = term if acc is None else acc + term
    o_ref[...] = acc


def _merge(ya, yb, yc, w_branch, layer, proj, *, tm, tn):
    n = ya.shape[0]
    y_spec = pl.BlockSpec((tm, BRANCH_WIDTH), lambda i, j: (i, 0))

    def gate_spec(b):
        return pl.BlockSpec((tm, tn), lambda i, j: (i, (COL_GATE + b * D_MODEL) // tn + j))

    pipelined = 3 * tm * BRANCH_WIDTH * 2 + 3 * BRANCH_WIDTH * tn * 2 + 4 * tm * tn * 4
    return pl.pallas_call(
        _merge_kernel,
        grid=(n // tm, D_MODEL // tn),
        in_specs=[y_spec, y_spec, y_spec,
                  pl.BlockSpec((None, N_BRANCH, BRANCH_WIDTH, tn), lambda i, j: (layer, 0, 0, j)),
                  gate_spec(0), gate_spec(1), gate_spec(2)],
        out_specs=pl.BlockSpec((tm, tn), lambda i, j: (i, j)),
        out_shape=jax.ShapeDtypeStruct((n, D_MODEL), F32),
        compiler_params=pltpu.CompilerParams(
            dimension_semantics=("parallel", "parallel"),
            vmem_limit_bytes=_vmem_limit(pipelined, 4 * tm * tn * 4)),
        name="merge",
    )(ya, yb, yc, w_branch, proj, proj, proj)


def _top_values(s, count):
    rows = []
    rank = jnp.full(s.shape, float(count), F32)
    for r in range(count):
        m = jnp.max(s, axis=0, keepdims=True)
        rows.append(m)
        hit = s == m
        rank = jnp.where(hit, float(r), rank)
        s = jnp.where(hit, -jnp.inf, s)
    return jnp.concatenate(rows, axis=0), rank


def _peer_select_kernel(q_ref, keys_ref, n_ref, w1_ref, r2_ref, w2_ref):
    nt = lax.dot_general
    contract_last = (((1,), (1,)), ((), ()))
    for h in range(PEER_HEADS):
        c0 = h * PEER_QDIM
        q1 = q_ref[:, c0:c0 + PEER_HALF].astype(BF16)
        q2 = q_ref[:, c0 + PEER_HALF:c0 + PEER_QDIM].astype(BF16)
        s1 = nt(keys_ref[0, h], q1, contract_last, preferred_element_type=F32)
        s2 = nt(keys_ref[1, h], q2, contract_last, preferred_element_type=F32)
        a, _ = _top_values(s1, PEER_TOPK)
        b, rank2 = _top_values(s2, PEER_TOPK)
        cand = [a[0:1] + b]
        cand += [a[p:p + 1] + b[0:SUBLANES] for p in range(1, SUBLANES)]
        cand.append(a[SUBLANES:] + b[0:1])
        cand = jnp.concatenate(cand, axis=0)
        work = cand
        for _ in range(PEER_TOPK - 1):
            m = jnp.max(work, axis=0, keepdims=True)
            work = jnp.where(work == m, -jnp.inf, work)
        thr = jnp.max(work, axis=0, keepdims=True)
        top = a[0:1] + b[0:1]
        z = jnp.sum(jnp.where(cand >= thr, jnp.exp(cand - top), 0.0), axis=0, keepdims=True)
        count = jnp.zeros(s1.shape, F32)
        for r in range(PEER_TOPK):
            count = count + jnp.where(s1 + b[r:r + 1] >= thr, 1.0, 0.0)
        n_ref[h] = count[:, None, :]
        w1_ref[h] = jnp.exp(s1 - a[0:1])[:, None, :]
        packed = (N_KEYS // BF16_ROWS, BF16_ROWS, s2.shape[1])
        r2_ref[h] = rank2.astype(BF16).reshape(packed)
        w2_ref[h] = (jnp.exp(s2 - b[0:1]) / z).astype(BF16).reshape(packed)


def _peer_select(q, keys, layer, *, tm):
    n = q.shape[0]
    rows = jax.ShapeDtypeStruct((PEER_HEADS, N_KEYS, 1, n), F32)
    rows_spec = pl.BlockSpec((PEER_HEADS, N_KEYS, 1, tm), lambda i: (0, 0, 0, i))
    big = jax.ShapeDtypeStruct((PEER_HEADS, N_KEYS // BF16_ROWS, BF16_ROWS, n), BF16)
    big_spec = pl.BlockSpec((PEER_HEADS, N_KEYS // BF16_ROWS, BF16_ROWS, tm), lambda i: (0, 0, 0, i))
    return pl.pallas_call(
        _peer_select_kernel,
        grid=(n // tm,),
        in_specs=[pl.BlockSpec((tm, PEER_HEADS * PEER_QDIM), lambda i: (i, 0)),
                  pl.BlockSpec((None, 2, PEER_HEADS, N_KEYS, PEER_HALF), lambda i: (layer, 0, 0, 0, 0))],
        out_specs=[rows_spec, rows_spec, big_spec, big_spec],
        out_shape=[rows, rows, big, big],
        compiler_params=pltpu.CompilerParams(
            dimension_semantics=("parallel",),
            vmem_limit_bytes=_vmem_limit(tm * PEER_HEADS * PEER_QDIM * 4 + 4 * PEER_HEADS * N_KEYS * tm * 4,
                                         16 << 20)),
        name="peer_select",
    )(q, keys)


def _peer_stage(u_ref, vt_ref, n_ref, w1_ref, r2_ref, w2_ref, xn_ref, acc_ref, h_new, h_old):
    te, tm = u_ref.shape[0], xn_ref.shape[0]
    packed = (N_KEYS // BF16_ROWS, BF16_ROWS, LANES)

    def row(ref, hd, i, ls):
        return jnp.broadcast_to(ref[hd, i, :, ls], (BF16_ROWS, LANES)).astype(BF16)

    def gate_block(i, c):
        rs = slice(i * N_KEYS, (i + 1) * N_KEYS)
        ls = slice(c * LANES, (c + 1) * LANES)
        gate = None
        for hd in range(PEER_HEADS):
            margin = jnp.maximum(row(n_ref, hd, i, ls) - r2_ref[hd, :, :, ls], 0.0)
            term = jnp.minimum(margin, w2_ref[hd, :, :, ls]) * row(w1_ref, hd, i, ls)
            gate = term if gate is None else gate + term
        act = _gelu(h_old[rs, ls]).reshape(packed).astype(BF16) * gate
        return act.reshape(N_KEYS, LANES)

    def output_piece(t):
        cols = [jnp.concatenate([gate_block(i, c) for i in range(te // N_KEYS)], axis=0)
                for c in range(t // LANES, (t + MXU_TILE) // LANES)]
        a = jnp.concatenate(cols, axis=1)
        acc_ref[:, t:t + MXU_TILE] += jnp.dot(vt_ref[...], a, preferred_element_type=F32)

    def hidden_piece(r):
        rs = slice(r, r + MXU_TILE)
        h_new[rs, :] = lax.dot_general(u_ref[rs, :], xn_ref[...], (((1,), (1,)), ((), ())),
                                       preferred_element_type=F32)

    hidden_per_output = (te // MXU_TILE) // (tm // MXU_TILE)
    assert hidden_per_output * tm == te
    for k in range(tm // MXU_TILE):
        if h_new is not None:
            for r in range(hidden_per_output):
                hidden_piece((k * hidden_per_output + r) * MXU_TILE)
        if h_old is not None:
            output_piece(k * MXU_TILE)


def _peer_ffn_kernel(x_ref, g_ref, u_ref, vt_ref, n_ref, w1_ref, r2_ref, w2_ref, o_ref,
                     xn_ref, acc_ref, h0_ref, h1_ref, *, tiles):
    s = pl.program_id(1)
    stage = functools.partial(_peer_stage, u_ref, vt_ref, n_ref, w1_ref, r2_ref, w2_ref, xn_ref, acc_ref)
    middle = (s > 0) & (s < tiles)

    @pl.when(s == 0)
    def _first_step():
        xn_ref[...] = _rms(x_ref[...], g_ref[...]).astype(BF16)
        acc_ref[...] = jnp.zeros_like(acc_ref)
        stage(h0_ref, None)

    @pl.when(middle & (s % 2 == 0))
    def _even_step():
        stage(h0_ref, h1_ref)

    @pl.when(middle & (s % 2 == 1))
    def _odd_step():
        stage(h1_ref, h0_ref)

    @pl.when(s == tiles)
    def _last_step():
        stage(None, h1_ref if tiles % 2 == 0 else h0_ref)
        o_ref[...] = x_ref[...] + acc_ref[...].T


PEER_PIPELINE_LAG = 1


def _peer_ffn(x, gain, u_bf, vt_bf, layer, count, w1, rank2, w2, *, tm, te):
    n = x.shape[0]
    slabs = te // N_KEYS
    tiles = N_EXPERTS // te
    clip = lambda s, lag: jnp.clip(s - lag, 0, tiles - 1)
    row_spec = pl.BlockSpec((PEER_HEADS, slabs, 1, tm), lambda t, s: (0, clip(s, PEER_PIPELINE_LAG), 0, t))
    full_spec = pl.BlockSpec((PEER_HEADS, N_KEYS // BF16_ROWS, BF16_ROWS, tm), lambda t, s: (0, 0, 0, t))
    pipelined = (2 * tm * D_MODEL * 4 + 2 * te * D_MODEL * 2
                 + 2 * PEER_HEADS * (slabs * SUBLANES * 4 + N_KEYS * 2) * tm)
    resident = tm * D_MODEL * 2 + D_MODEL * tm * 4 + 2 * te * tm * 4 + D_MODEL * tm * 4
    return pl.pallas_call(
        functools.partial(_peer_ffn_kernel, tiles=tiles),
        grid=(n // tm, tiles + PEER_PIPELINE_LAG),
        in_specs=[
            pl.BlockSpec((tm, D_MODEL), lambda t, s: (t, 0)),
            pl.BlockSpec((1, D_MODEL), lambda t, s: (0, 0)),
            pl.BlockSpec((None, te, D_MODEL), lambda t, s: (layer, clip(s, 0), 0)),
            pl.BlockSpec((None, D_MODEL, te), lambda t, s: (layer, 0, clip(s, PEER_PIPELINE_LAG))),
            row_spec, row_spec, full_spec, full_spec,
        ],
        out_specs=pl.BlockSpec((tm, D_MODEL), lambda t, s: (t, 0)),
        out_shape=jax.ShapeDtypeStruct((n, D_MODEL), F32),
        scratch_shapes=[pltpu.VMEM((tm, D_MODEL), BF16),
                        pltpu.VMEM((D_MODEL, tm), F32),
                        pltpu.VMEM((te, tm), F32), pltpu.VMEM((te, tm), F32)],
        compiler_params=pltpu.CompilerParams(
            dimension_semantics=("parallel", "arbitrary"),
            vmem_limit_bytes=_vmem_limit(pipelined, resident)),
        name="peer_ffn",
    )(x, gain.reshape(1, D_MODEL), u_bf, vt_bf, count, w1, rank2, w2)


IN_PROJ_TN = IN_COLS // 7


def kernel(x, norm1_g, w_in, ssm_a_re, ssm_a_im, ssm_b_re, ssm_b_im, ssm_c_re, ssm_c_im, ssm_d, ssm_log_dt,
           w_glu, b_glu, q_norm_g, k_norm_g, attn_sinks, sgu_ln_g, sgu_ln_b, sgu_w, sgu_b, w_branch, w_out,
           norm2_g, w_query, peer_keys, peer_u, peer_v):
    batch, seq, d_model = x.shape
    assert d_model == D_MODEL and seq % S5_CHUNK == 0
    xt = x.reshape(batch * seq, D_MODEL)
    w_in, w_glu, w_branch, w_out, w_query, peer_keys, peer_u = (
        w.astype(BF16) for w in (w_in, w_glu, w_branch, w_out, w_query, peer_keys, peer_u))
    peer_vt = jnp.swapaxes(peer_v, 1, 2).astype(BF16)
    for l in range(DEPTH):
        proj = _matmul(xt, w_in, l, gain=norm1_g[l], tm=1024, tn=IN_PROJ_TN)
        bmat, cmat, pw = _s5_discretise(ssm_a_re[l], ssm_a_im[l], ssm_b_re[l], ssm_b_im[l],
                                        ssm_c_re[l], ssm_c_im[l], ssm_log_dt[l])
        y_ssm = _s5_scan(proj, ssm_d[l], bmat, cmat, pw, batch=batch, seq=seq)
        y_a = _s5_glu(y_ssm, w_glu, l, b_glu[l], tm=512)
        y_b = _swa(proj, q_norm_g[l], k_norm_g[l], attn_sinks[l], batch=batch, seq=seq)
        y_c = _sgu(proj, sgu_ln_g[l], sgu_ln_b[l], sgu_w[l], sgu_b[l], tm=256)
        merged = _merge(y_a, y_b, y_c, w_branch, l, proj, tm=1024, tn=512)
        xt = _matmul(merged, w_out, l, residual=xt, tm=512, tn=D_MODEL)
        q = _matmul(xt, w_query, l, gain=norm2_g[l], tm=512, tn=D_MODEL)
        count, w1, rank2, w2 = _peer_select(q, peer_keys, l, tm=256)
        xt = _peer_ffn(xt, norm2_g[l], peer_u, peer_vt, l, count, w1, rank2, w2, tm=512, te=1024)
    return xt.reshape(batch, seq, D_MODEL)
```

```python
import functools
import math

import jax
import jax.numpy as jnp
from jax import lax
from jax.experimental import pallas as pl
from jax.experimental.pallas import tpu as pltpu

F32 = jnp.float32
BF16 = jnp.bfloat16

D_MODEL = 2048
DEPTH = 2
SSM_WIDTH = 1024
SSM_GROUP = 16
SSM_GROUPS = SSM_WIDTH // SSM_GROUP
SSM_STATE = 64
HEAD_DIM = 64
N_Q_HEADS = 16
N_KV_HEADS = 4
Q_PER_KV = N_Q_HEADS // N_KV_HEADS
WINDOW = 128
ATTN_WIDTH = N_Q_HEADS * HEAD_DIM
KV_WIDTH = N_KV_HEADS * HEAD_DIM
SGU_WIDTH = 1024
SGU_HEADS = 8
SGU_HEAD_DIM = SGU_WIDTH // SGU_HEADS
SGU_CHUNK = 128
N_BRANCH = 3
BRANCH_WIDTH = 1024
PEER_HEADS = 8
PEER_QDIM = 256
PEER_HALF = PEER_QDIM // 2
N_KEYS = 128
N_EXPERTS = N_KEYS * N_KEYS
PEER_TOPK = 16
EPS = 1e-6

V7X_VMEM_BYTES = 64 * 1024 * 1024
SUBLANES = 8
LANES = 128
BF16_ROWS = 2 * SUBLANES
MXU_TILE = 256

COL_SSM = 0
COL_Q = COL_SSM + SSM_WIDTH
COL_K = COL_Q + ATTN_WIDTH
COL_V = COL_K + KV_WIDTH
COL_SGU = COL_V + KV_WIDTH
COL_GATE = COL_SGU + 2 * SGU_WIDTH
IN_COLS = COL_GATE + N_BRANCH * D_MODEL

S5_GB = 16
S5_NGB = SSM_GROUPS // S5_GB
S5_IN = S5_GB * SSM_GROUP
S5_ST = S5_GB * SSM_STATE
S5_CHUNK = 512

SQRT_HALF = math.sqrt(0.5)


def _vmem_limit(pipelined_bytes, resident_bytes):
    need = 2 * pipelined_bytes + resident_bytes + (4 << 20)
    return int(min(need, V7X_VMEM_BYTES - (8 << 20)))


def _gelu(x):
    return 0.5 * x * (1.0 + lax.erf(x * SQRT_HALF))


def _rms(x, g):
    return x * lax.rsqrt(jnp.mean(x * x, axis=-1, keepdims=True) + EPS) * g


def _matmul_kernel(*refs, norm, residual):
    refs = list(refs)
    a_ref = refs.pop(0)
    g_ref = refs.pop(0) if norm else None
    w_ref = refs.pop(0)
    r_ref = refs.pop(0) if residual else None
    o_ref, abf_ref = refs

    @pl.when(pl.program_id(1) == 0)
    def _prepare_lhs():
        a = a_ref[...]
        if norm:
            a = _rms(a, g_ref[...])
        abf_ref[...] = a.astype(BF16)

    acc = jnp.dot(abf_ref[...], w_ref[...], preferred_element_type=F32)
    if residual:
        acc = r_ref[...] + acc
    o_ref[...] = acc


def _matmul(a, w, layer, *, gain=None, residual=None, tm, tn):
    m, k = a.shape
    n = w.shape[2]
    assert m % tm == 0 and n % tn == 0
    operands = [a]
    in_specs = [pl.BlockSpec((tm, k), lambda i, j: (i, 0))]
    if gain is not None:
        operands.append(gain.reshape(1, k))
        in_specs.append(pl.BlockSpec((1, k), lambda i, j: (0, 0)))
    operands.append(w)
    in_specs.append(pl.BlockSpec((None, k, tn), lambda i, j: (layer, 0, j)))
    if residual is not None:
        operands.append(residual)
        in_specs.append(pl.BlockSpec((tm, tn), lambda i, j: (i, j)))
    pipelined = tm * k * 4 + k * tn * 2 + tm * tn * 4 * (2 if residual is not None else 1)
    return pl.pallas_call(
        functools.partial(_matmul_kernel, norm=gain is not None, residual=residual is not None),
        grid=(m // tm, n // tn),
        in_specs=in_specs,
        out_specs=pl.BlockSpec((tm, tn), lambda i, j: (i, j)),
        out_shape=jax.ShapeDtypeStruct((m, n), F32),
        scratch_shapes=[pltpu.VMEM((tm, k), BF16)],
        compiler_params=pltpu.CompilerParams(
            dimension_semantics=("parallel", "arbitrary"),
            vmem_limit_bytes=_vmem_limit(pipelined, tm * k * 2 + tm * k * 4)),
        name="matmul",
    )(*operands)


def _s5_kernel(u_ref, d_ref, bm_ref, cm_ref, pw_ref, o_ref, x_ref, carry_ref):
    nb, tc = u_ref.shape[0], u_ref.shape[1]
    ng = tc // SUBLANES

    @pl.when(pl.program_id(1) == 0)
    def _reset_state():
        carry_ref[...] = jnp.zeros_like(carry_ref)

    for b in range(nb):
        bu = jnp.dot(u_ref[b].astype(BF16), bm_ref[0], preferred_element_type=F32)
        re = bu[:, :S5_ST].reshape(ng, SUBLANES, S5_ST)
        im = bu[:, S5_ST:].reshape(ng, SUBLANES, S5_ST)
        for idx, k in enumerate((1, 2, 4)):
            lre = pw_ref[0, 2 * idx]
            lim = pw_ref[0, 2 * idx + 1]
            sre = pltpu.roll(re, k, 1)
            sim = pltpu.roll(im, k, 1)
            re, im = re + (lre * sre - lim * sim), im + (lre * sim + lim * sre)
        x_ref[b, :, :S5_ST] = re.reshape(tc, S5_ST)
        x_ref[b, :, S5_ST:] = im.reshape(tc, S5_ST)

    cyre = pw_ref[0, 6]
    cyim = pw_ref[0, 7]

    def _carry_group(g, carry):
        r0 = pl.multiple_of(g * SUBLANES, SUBLANES)
        last = SUBLANES - 1
        out = []
        for b in range(nb):
            cr, ci = carry[2 * b], carry[2 * b + 1]
            xr = x_ref[b, pl.ds(r0, SUBLANES), :S5_ST] + (cyre * cr - cyim * ci)
            xi = x_ref[b, pl.ds(r0, SUBLANES), S5_ST:] + (cyre * ci + cyim * cr)
            x_ref[b, pl.ds(r0, SUBLANES), :S5_ST] = xr
            x_ref[b, pl.ds(r0, SUBLANES), S5_ST:] = xi
            out += [jnp.broadcast_to(xr[last:, :], (SUBLANES, S5_ST)),
                    jnp.broadcast_to(xi[last:, :], (SUBLANES, S5_ST))]
        return tuple(out)

    carry = lax.fori_loop(0, ng, _carry_group, tuple(carry_ref[k] for k in range(2 * nb)))
    for k in range(2 * nb):
        carry_ref[k] = carry[k]

    for b in range(nb):
        y = jnp.dot(x_ref[b].astype(BF16), cm_ref[0], preferred_element_type=F32)
        o_ref[b] = y + d_ref[...] * u_ref[b]


def _s5_discretise(a_re, a_im, b_re, b_im, c_re, c_im, log_dt):
    depth = a_re.shape[0]
    dt = jnp.exp(log_dt)[..., None]

    def a_pow(k):
        mag = jnp.exp(a_re * dt * k)
        return mag * jnp.cos(a_im * dt * k), mag * jnp.sin(a_im * dt * k)

    abar_re, abar_im = a_pow(1.0)
    num_re, num_im = abar_re - 1.0, abar_im
    den = a_re * a_re + a_im * a_im
    coef_re = (num_re * a_re + num_im * a_im) / den
    coef_im = (num_im * a_re - num_re * a_im) / den
    bbar_re = coef_re[..., None] * b_re - coef_im[..., None] * b_im
    bbar_im = coef_re[..., None] * b_im + coef_im[..., None] * b_re

    eye = jnp.eye(S5_GB, dtype=F32)

    def in_block(b):
        b = b.reshape(depth, S5_NGB, S5_GB, SSM_STATE, SSM_GROUP)
        return jnp.einsum('lbgph,gk->lbghkp', b, eye).reshape(depth, S5_NGB, S5_IN, S5_ST)

    def out_block(c):
        c = c.reshape(depth, S5_NGB, S5_GB, SSM_GROUP, SSM_STATE)
        return jnp.einsum('lbghp,gk->lbgpkh', c, eye).reshape(depth, S5_NGB, S5_ST, S5_IN)

    bmat = jnp.concatenate([in_block(bbar_re), in_block(bbar_im)], axis=3).astype(BF16)
    cmat = jnp.concatenate([out_block(c_re), out_block(-c_im)], axis=2).astype(BF16)

    rows = jnp.arange(SUBLANES, dtype=F32)[:, None, None, None]
    tiles = []
    for k in (1, 2, 4):
        pr, pi = a_pow(float(k))
        tiles += [jnp.where(rows >= k, pr[None], 0.0), jnp.where(rows >= k, pi[None], 0.0)]
    mag = jnp.exp(a_re[None] * dt[None] * (rows + 1.0))
    tiles += [mag * jnp.cos(a_im[None] * dt[None] * (rows + 1.0)), mag * jnp.sin(a_im[None] * dt[None] * (rows + 1.0))]
    pw = jnp.stack(tiles).reshape(len(tiles), SUBLANES, depth, S5_NGB, S5_ST).transpose(2, 3, 0, 1, 4)
    return bmat, cmat, pw


def _s5_scan(proj, ssm_d, bmat, cmat, pw, layer, *, batch, seq):
    nc = seq // S5_CHUNK
    proj = proj.reshape(batch, seq, proj.shape[1])
    pipelined = batch * S5_CHUNK * S5_IN * 4 * 2 + S5_IN * 2 * S5_ST * 2 * 2 + 8 * SUBLANES * S5_ST * 4
    resident = batch * S5_CHUNK * 2 * S5_ST * 4 * 4
    out = pl.pallas_call(
        _s5_kernel,
        grid=(S5_NGB, nc),
        in_specs=[
            pl.BlockSpec((batch, S5_CHUNK, S5_IN), lambda g, c: (0, c, COL_SSM // S5_IN + g)),
            pl.BlockSpec((None, 1, S5_IN), lambda g, c: (layer, 0, g)),
            pl.BlockSpec((None, 1, S5_IN, 2 * S5_ST), lambda g, c: (layer, g, 0, 0)),
            pl.BlockSpec((None, 1, 2 * S5_ST, S5_IN), lambda g, c: (layer, g, 0, 0)),
            pl.BlockSpec((None, 1, 8, SUBLANES, S5_ST), lambda g, c: (layer, g, 0, 0, 0)),
        ],
        out_specs=pl.BlockSpec((batch, S5_CHUNK, S5_IN), lambda g, c: (0, c, g)),
        out_shape=jax.ShapeDtypeStruct((batch, seq, SSM_WIDTH), F32),
        scratch_shapes=[pltpu.VMEM((batch, S5_CHUNK, 2 * S5_ST), F32),
                        pltpu.VMEM((2 * batch, SUBLANES, S5_ST), F32)],
        compiler_params=pltpu.CompilerParams(
            dimension_semantics=("parallel", "arbitrary"),
            vmem_limit_bytes=_vmem_limit(pipelined, resident)),
        name="s5_scan",
    )(proj, ssm_d[:, None, :], bmat, cmat, pw)
    return out.reshape(batch * seq, SSM_WIDTH)


def _s5_glu_kernel(y_ref, w_ref, b_ref, o_ref):
    y = _gelu(y_ref[...])
    z = jnp.dot(y.astype(BF16), w_ref[...], preferred_element_type=F32) + b_ref[...]
    o_ref[...] = (y * jax.nn.sigmoid(z)).astype(o_ref.dtype)


def _s5_glu(y, w_glu, layer, b_glu, *, tm):
    n = y.shape[0]
    return pl.pallas_call(
        _s5_glu_kernel,
        grid=(n // tm,),
        in_specs=[pl.BlockSpec((tm, SSM_WIDTH), lambda i: (i, 0)),
                  pl.BlockSpec((None, SSM_WIDTH, SSM_WIDTH), lambda i: (layer, 0, 0)),
                  pl.BlockSpec((1, SSM_WIDTH), lambda i: (0, 0))],
        out_specs=pl.BlockSpec((tm, SSM_WIDTH), lambda i: (i, 0)),
        out_shape=jax.ShapeDtypeStruct((n, SSM_WIDTH), BF16),
        compiler_params=pltpu.CompilerParams(
            dimension_semantics=("parallel",),
            vmem_limit_bytes=_vmem_limit(2 * tm * SSM_WIDTH * 4 + SSM_WIDTH * SSM_WIDTH * 2,
                                         4 * tm * SSM_WIDTH * 4)),
        name="s5_glu",
    )(y, w_glu, b_glu.reshape(1, SSM_WIDTH))


def _swa_kernel(sink_ref, q_ref, kp_ref, kc_ref, vp_ref, vc_ref, qg_ref, kg_ref, o_ref, *, blocks_per_seq):
    first = (pl.program_id(0) % blocks_per_seq) == 0
    q = q_ref[...]
    k = jnp.concatenate([kp_ref[...], kc_ref[...]], axis=0)
    v = jnp.concatenate([vp_ref[...], vc_ref[...]], axis=0)
    rows = Q_PER_KV * WINDOW
    qpos = (lax.broadcasted_iota(jnp.int32, (rows, 2 * WINDOW), 0) & (WINDOW - 1)) + WINDOW
    kpos = lax.broadcasted_iota(jnp.int32, (rows, 2 * WINDOW), 1)
    diff = qpos - kpos
    first_key = jnp.where(first, WINDOW, 0)
    valid = (diff >= 0) & (diff < WINDOW) & (kpos >= first_key)
    groups = [range(kv * Q_PER_KV, (kv + 1) * Q_PER_KV) for kv in range(N_KV_HEADS)]
    qb = jnp.stack([jnp.concatenate([q[:, h * HEAD_DIM:(h + 1) * HEAD_DIM] for h in grp], axis=0)
                    for grp in groups])
    kb = jnp.stack([k[:, kv * HEAD_DIM:(kv + 1) * HEAD_DIM] for kv in range(N_KV_HEADS)])
    vb = jnp.stack([v[:, kv * HEAD_DIM:(kv + 1) * HEAD_DIM] for kv in range(N_KV_HEADS)]).astype(BF16)
    qb = _rms(qb, qg_ref[...]).astype(BF16)
    kb = _rms(kb, kg_ref[...]).astype(BF16)
    sink = jnp.stack([jnp.concatenate([jnp.full((WINDOW, 1), sink_ref[h], F32) for h in grp], axis=0)
                      for grp in groups])
    s = jnp.einsum('gqd,gkd->gqk', qb, kb, preferred_element_type=F32)
    s = jnp.where(valid[None], s * (HEAD_DIM ** -0.5), -1e30)
    m = jnp.maximum(jnp.max(s, axis=-1, keepdims=True), sink)
    p = jnp.exp(s - m)
    denom = jnp.sum(p, axis=-1, keepdims=True) + jnp.exp(sink - m)
    o = jnp.einsum('gqk,gkd->gqd', p.astype(BF16), vb, preferred_element_type=F32) / denom
    outs = [o[kv, g * WINDOW:(g + 1) * WINDOW] for kv in range(N_KV_HEADS) for g in range(Q_PER_KV)]
    o_ref[...] = jnp.concatenate(outs, axis=-1).astype(o_ref.dtype)


def _swa(proj, q_g, k_g, sinks, *, batch, seq):
    n = batch * seq
    nb = seq // WINDOW
    prev = lambda r: jnp.maximum(r - 1, 0)
    pipelined = WINDOW * (2 * ATTN_WIDTH + 4 * KV_WIDTH) * 4
    return pl.pallas_call(
        functools.partial(_swa_kernel, blocks_per_seq=nb),
        grid=(n // WINDOW,),
        in_specs=[
            pl.BlockSpec(memory_space=pltpu.SMEM),
            pl.BlockSpec((WINDOW, ATTN_WIDTH), lambda r: (r, COL_Q // ATTN_WIDTH)),
            pl.BlockSpec((WINDOW, KV_WIDTH), lambda r: (prev(r), COL_K // KV_WIDTH)),
            pl.BlockSpec((WINDOW, KV_WIDTH), lambda r: (r, COL_K // KV_WIDTH)),
            pl.BlockSpec((WINDOW, KV_WIDTH), lambda r: (prev(r), COL_V // KV_WIDTH)),
            pl.BlockSpec((WINDOW, KV_WIDTH), lambda r: (r, COL_V // KV_WIDTH)),
            pl.BlockSpec((1, HEAD_DIM), lambda r: (0, 0)),
            pl.BlockSpec((1, HEAD_DIM), lambda r: (0, 0)),
        ],
        out_specs=pl.BlockSpec((WINDOW, ATTN_WIDTH), lambda r: (r, 0)),
        out_shape=jax.ShapeDtypeStruct((n, ATTN_WIDTH), BF16),
        compiler_params=pltpu.CompilerParams(
            dimension_semantics=("parallel",),
            vmem_limit_bytes=_vmem_limit(pipelined, 16 << 20)),
        name="swa",
    )(sinks, proj, proj, proj, proj, proj, q_g.reshape(1, HEAD_DIM), k_g.reshape(1, HEAD_DIM))


def _sgu_kernel(*refs):
    z_refs = refs[:SGU_PARTS]
    g_ref, b_ref, w_ref, bs_ref, o_ref = refs[SGU_PARTS:]
    tm = o_ref.shape[0]
    z = _gelu(jnp.concatenate([r[...] for r in z_refs], axis=1))
    u = z[:, :SGU_WIDTH]
    v = z[:, SGU_WIDTH:]
    vc = v - jnp.mean(v, axis=-1, keepdims=True)
    vn = vc * lax.rsqrt(jnp.mean(vc * vc, axis=-1, keepdims=True) + EPS) * g_ref[...] + b_ref[...]
    vn = vn.astype(BF16)
    row = lax.broadcasted_iota(jnp.int32, (SGU_CHUNK, SGU_CHUNK), 0)
    col = lax.broadcasted_iota(jnp.int32, (SGU_CHUNK, SGU_CHUNK), 1)
    for h in range(SGU_HEADS):
        w_h = jnp.where(row >= col, w_ref[h], 0.0).astype(BF16)
        cs = slice(h * SGU_HEAD_DIM, (h + 1) * SGU_HEAD_DIM)
        for c in range(tm // SGU_CHUNK):
            rs = slice(c * SGU_CHUNK, (c + 1) * SGU_CHUNK)
            mixed = jnp.dot(w_h, vn[rs, cs], preferred_element_type=F32) + bs_ref[:, h:h + 1]
            o_ref[rs, cs] = (u[rs, cs] * mixed).astype(o_ref.dtype)


SGU_PART = math.gcd(COL_SGU, 2 * SGU_WIDTH)
SGU_PARTS = 2 * SGU_WIDTH // SGU_PART


def _sgu(proj, ln_g, ln_b, w_s, b_s, *, tm):
    n = proj.shape[0]
    z_specs = [pl.BlockSpec((tm, SGU_PART), lambda i, p=p: (i, COL_SGU // SGU_PART + p)) for p in range(SGU_PARTS)]
    return pl.pallas_call(
        _sgu_kernel,
        grid=(n // tm,),
        in_specs=z_specs + [
            pl.BlockSpec((1, SGU_WIDTH), lambda i: (0, 0)),
            pl.BlockSpec((1, SGU_WIDTH), lambda i: (0, 0)),
            pl.BlockSpec((SGU_HEADS, SGU_CHUNK, SGU_CHUNK), lambda i: (0, 0, 0)),
            pl.BlockSpec((SGU_CHUNK, SGU_HEADS), lambda i: (0, 0)),
        ],
        out_specs=pl.BlockSpec((tm, SGU_WIDTH), lambda i: (i, 0)),
        out_shape=jax.ShapeDtypeStruct((n, SGU_WIDTH), BF16),
        compiler_params=pltpu.CompilerParams(
            dimension_semantics=("parallel",),
            vmem_limit_bytes=_vmem_limit(tm * 3 * SGU_WIDTH * 4, 6 * tm * 2 * SGU_WIDTH * 4)),
        name="sgu",
    )(*([proj] * SGU_PARTS), ln_g.reshape(1, SGU_WIDTH), ln_b.reshape(1, SGU_WIDTH), w_s, b_s.T)


def _merge_kernel(ya_ref, yb_ref, yc_ref, w_ref, ga_ref, gb_ref, gc_ref, o_ref):
    acc = None
    for n, (y_ref, gate_ref) in enumerate(((ya_ref, ga_ref), (yb_ref, gb_ref), (yc_ref, gc_ref))):
        branch = jnp.dot(y_ref[...], w_ref[n], preferred_element_type=F32)
        term = jax.nn.sigmoid(gate_ref[...]) * branch
        acc = term if acc is None else acc + term
    o_ref[...] = acc


def _merge(ya, yb, yc, w_branch, layer, proj, *, tm, tn):
    n = ya.shape[0]
    y_spec = pl.BlockSpec((tm, BRANCH_WIDTH), lambda i, j: (i, 0))

    def gate_spec(b):
        return pl.BlockSpec((tm, tn), lambda i, j: (i, (COL_GATE + b * D_MODEL) // tn + j))

    pipelined = 3 * tm * BRANCH_WIDTH * 2 + 3 * BRANCH_WIDTH * tn * 2 + 4 * tm * tn * 4
    return pl.pallas_call(
        _merge_kernel,
        grid=(n // tm, D_MODEL // tn),
        in_specs=[y_spec, y_spec, y_spec,
                  pl.BlockSpec((None, N_BRANCH, BRANCH_WIDTH, tn), lambda i, j: (layer, 0, 0, j)),
                  gate_spec(0), gate_spec(1), gate_spec(2)],
        out_specs=pl.BlockSpec((tm, tn), lambda i, j: (i, j)),
        out_shape=jax.ShapeDtypeStruct((n, D_MODEL), F32),
        compiler_params=pltpu.CompilerParams(
            dimension_semantics=("parallel", "parallel"),
            vmem_limit_bytes=_vmem_limit(pipelined, 4 * tm * tn * 4)),
        name="merge",
    )(ya, yb, yc, w_branch, proj, proj, proj)


def _top_values(s, count):
    rows = []
    rank = jnp.full(s.shape, float(count), F32)
    for r in range(count):
        m = jnp.max(s, axis=0, keepdims=True)
        rows.append(m)
        hit = s == m
        rank = jnp.where(hit, float(r), rank)
        s = jnp.where(hit, -jnp.inf, s)
    return jnp.concatenate(rows, axis=0), rank


def _peer_select_kernel(q_ref, keys_ref, n_ref, w1_ref, r2_ref, w2_ref):
    nt = lax.dot_general
    contract_last = (((1,), (1,)), ((), ()))
    for h in range(PEER_HEADS):
        c0 = h * PEER_QDIM
        q1 = q_ref[:, c0:c0 + PEER_HALF].astype(BF16)
        q2 = q_ref[:, c0 + PEER_HALF:c0 + PEER_QDIM].astype(BF16)
        s1 = nt(keys_ref[0, h], q1, contract_last, preferred_element_type=F32)
        s2 = nt(keys_ref[1, h], q2, contract_last, preferred_element_type=F32)
        a, _ = _top_values(s1, PEER_TOPK)
        b, rank2 = _top_values(s2, PEER_TOPK)
        cand = [a[0:1] + b]
        cand += [a[p:p + 1] + b[0:SUBLANES] for p in range(1, SUBLANES)]
        cand.append(a[SUBLANES:] + b[0:1])
        cand = jnp.concatenate(cand, axis=0)
        work = cand
        for _ in range(PEER_TOPK - 1):
            m = jnp.max(work, axis=0, keepdims=True)
            work = jnp.where(work == m, -jnp.inf, work)
        thr = jnp.max(work, axis=0, keepdims=True)
        top = a[0:1] + b[0:1]
        z = jnp.sum(jnp.where(cand >= thr, jnp.exp(cand - top), 0.0), axis=0, keepdims=True)
        count = jnp.zeros(s1.shape, F32)
        for r in range(PEER_TOPK):
            count = count + jnp.where(s1 + b[r:r + 1] >= thr, 1.0, 0.0)
        n_ref[h] = count
        w1_ref[h] = jnp.exp(s1 - a[0:1])
        packed = (N_KEYS // BF16_ROWS, BF16_ROWS, s2.shape[1])
        r2_ref[h] = rank2.astype(BF16).reshape(packed)
        w2_ref[h] = (jnp.exp(s2 - b[0:1]) / z).astype(BF16).reshape(packed)


def _peer_select(q, keys, layer, *, tm):
    n = q.shape[0]
    rows = jax.ShapeDtypeStruct((PEER_HEADS, N_KEYS, n), F32)
    rows_spec = pl.BlockSpec((PEER_HEADS, N_KEYS, tm), lambda i: (0, 0, i))
    big = jax.ShapeDtypeStruct((PEER_HEADS, N_KEYS // BF16_ROWS, BF16_ROWS, n), BF16)
    big_spec = pl.BlockSpec((PEER_HEADS, N_KEYS // BF16_ROWS, BF16_ROWS, tm), lambda i: (0, 0, 0, i))
    return pl.pallas_call(
        _peer_select_kernel,
        grid=(n // tm,),
        in_specs=[pl.BlockSpec((tm, PEER_HEADS * PEER_QDIM), lambda i: (i, 0)),
                  pl.BlockSpec((None, 2, PEER_HEADS, N_KEYS, PEER_HALF), lambda i: (layer, 0, 0, 0, 0))],
        out_specs=[rows_spec, rows_spec, big_spec, big_spec],
        out_shape=[rows, rows, big, big],
        compiler_params=pltpu.CompilerParams(
            dimension_semantics=("parallel",),
            vmem_limit_bytes=_vmem_limit(tm * PEER_HEADS * PEER_QDIM * 4 + 4 * PEER_HEADS * N_KEYS * tm * 4,
                                         16 << 20)),
        name="peer_select",
    )(q, keys)


def _peer_stage(u_ref, vt_ref, n_ref, w1_ref, r2_ref, w2_ref, xn_ref, acc_ref, h_new, h_old):
    te, tm = u_ref.shape[0], xn_ref.shape[0]
    packed = (N_KEYS // BF16_ROWS, BF16_ROWS, LANES)

    def row(ref, hd, i, ls):
        return jnp.broadcast_to(ref[hd, i:i + 1, ls], (BF16_ROWS, LANES)).astype(BF16)

    def gate_block(i, c):
        rs = slice(i * N_KEYS, (i + 1) * N_KEYS)
        ls = slice(c * LANES, (c + 1) * LANES)
        gate = None
        for hd in range(PEER_HEADS):
            margin = jnp.maximum(row(n_ref, hd, i, ls) - r2_ref[hd, :, :, ls], 0.0)
            term = jnp.minimum(margin, w2_ref[hd, :, :, ls]) * row(w1_ref, hd, i, ls)
            gate = term if gate is None else gate + term
        act = _gelu(h_old[rs, ls]).reshape(packed).astype(BF16) * gate
        return act.reshape(N_KEYS, LANES)

    def output_piece(t):
        cols = [jnp.concatenate([gate_block(i, c) for i in range(te // N_KEYS)], axis=0)
                for c in range(t // LANES, (t + MXU_TILE) // LANES)]
        a = jnp.concatenate(cols, axis=1)
        acc_ref[:, t:t + MXU_TILE] += jnp.dot(vt_ref[...], a, preferred_element_type=F32)

    def hidden_piece(r):
        rs = slice(r, r + MXU_TILE)
        h_new[rs, :] = lax.dot_general(u_ref[rs, :], xn_ref[...], (((1,), (1,)), ((), ())),
                                       preferred_element_type=F32)

    hidden_per_output = (te // MXU_TILE) // (tm // MXU_TILE)
    assert hidden_per_output * tm == te
    for k in range(tm // MXU_TILE):
        if h_new is not None:
            for r in range(hidden_per_output):
                hidden_piece((k * hidden_per_output + r) * MXU_TILE)
        if h_old is not None:
            output_piece(k * MXU_TILE)


def _peer_ffn_kernel(x_ref, g_ref, u_ref, vt_ref, n_ref, w1_ref, r2_ref, w2_ref, o_ref,
                     xn_ref, acc_ref, h0_ref, h1_ref, *, tiles):
    s = pl.program_id(1)
    stage = functools.partial(_peer_stage, u_ref, vt_ref, n_ref, w1_ref, r2_ref, w2_ref, xn_ref, acc_ref)
    middle = (s > 0) & (s < tiles)

    @pl.when(s == 0)
    def _first_step():
        xn_ref[...] = _rms(x_ref[...], g_ref[...]).astype(BF16)
        acc_ref[...] = jnp.zeros_like(acc_ref)
        stage(h0_ref, None)

    @pl.when(middle & (s % 2 == 0))
    def _even_step():
        stage(h0_ref, h1_ref)

    @pl.when(middle & (s % 2 == 1))
    def _odd_step():
        stage(h1_ref, h0_ref)

    @pl.when(s == tiles)
    def _last_step():
        stage(None, h1_ref if tiles % 2 == 0 else h0_ref)
        o_ref[...] = x_ref[...] + acc_ref[...].T


PEER_PIPELINE_LAG = 1


def _peer_ffn(x, gain, u_bf, vt_bf, layer, count, w1, rank2, w2, *, tm, te):
    n = x.shape[0]
    slabs = te // N_KEYS
    tiles = N_EXPERTS // te
    clip = lambda s, lag: jnp.clip(s - lag, 0, tiles - 1)
    assert slabs % SUBLANES == 0
    row_spec = pl.BlockSpec((PEER_HEADS, slabs, tm), lambda t, s: (0, clip(s, PEER_PIPELINE_LAG), t))
    full_spec = pl.BlockSpec((PEER_HEADS, N_KEYS // BF16_ROWS, BF16_ROWS, tm), lambda t, s: (0, 0, 0, t))
    pipelined = (2 * tm * D_MODEL * 4 + 2 * te * D_MODEL * 2
                 + 2 * PEER_HEADS * (slabs * SUBLANES * 4 + N_KEYS * 2) * tm)
    resident = tm * D_MODEL * 2 + D_MODEL * tm * 4 + 2 * te * tm * 4 + D_MODEL * tm * 4
    return pl.pallas_call(
        functools.partial(_peer_ffn_kernel, tiles=tiles),
        grid=(n // tm, tiles + PEER_PIPELINE_LAG),
        in_specs=[
            pl.BlockSpec((tm, D_MODEL), lambda t, s: (t, 0)),
            pl.BlockSpec((1, D_MODEL), lambda t, s: (0, 0)),
            pl.BlockSpec((None, te, D_MODEL), lambda t, s: (layer, clip(s, 0), 0)),
            pl.BlockSpec((None, D_MODEL, te), lambda t, s: (layer, 0, clip(s, PEER_PIPELINE_LAG))),
            row_spec, row_spec, full_spec, full_spec,
        ],
        out_specs=pl.BlockSpec((tm, D_MODEL), lambda t, s: (t, 0)),
        out_shape=jax.ShapeDtypeStruct((n, D_MODEL), F32),
        scratch_shapes=[pltpu.VMEM((tm, D_MODEL), BF16),
                        pltpu.VMEM((D_MODEL, tm), F32),
                        pltpu.VMEM((te, tm), F32), pltpu.VMEM((te, tm), F32)],
        compiler_params=pltpu.CompilerParams(
            dimension_semantics=("parallel", "arbitrary"),
            vmem_limit_bytes=_vmem_limit(pipelined, resident)),
        name="peer_ffn",
    )(x, gain.reshape(1, D_MODEL), u_bf, vt_bf, count, w1, rank2, w2)


IN_PROJ_TN = IN_COLS // 7


def kernel(x, norm1_g, w_in, ssm_a_re, ssm_a_im, ssm_b_re, ssm_b_im, ssm_c_re, ssm_c_im, ssm_d, ssm_log_dt,
           w_glu, b_glu, q_norm_g, k_norm_g, attn_sinks, sgu_ln_g, sgu_ln_b, sgu_w, sgu_b, w_branch, w_out,
           norm2_g, w_query, peer_keys, peer_u, peer_v):
    batch, seq, d_model = x.shape
    assert d_model == D_MODEL and seq % S5_CHUNK == 0
    xt = x.reshape(batch * seq, D_MODEL)
    w_in, w_glu, w_branch, w_out, w_query, peer_keys, peer_u = (
        w.astype(BF16) for w in (w_in, w_glu, w_branch, w_out, w_query, peer_keys, peer_u))
    peer_vt = jnp.swapaxes(peer_v, 1, 2).astype(BF16)
    bmat, cmat, pw = _s5_discretise(ssm_a_re, ssm_a_im, ssm_b_re, ssm_b_im, ssm_c_re, ssm_c_im, ssm_log_dt)
    for l in range(DEPTH):
        proj = _matmul(xt, w_in, l, gain=norm1_g[l], tm=1024, tn=IN_PROJ_TN)
        y_ssm = _s5_scan(proj, ssm_d, bmat, cmat, pw, l, batch=batch, seq=seq)
        y_a = _s5_glu(y_ssm, w_glu, l, b_glu[l], tm=512)
        y_b = _swa(proj, q_norm_g[l], k_norm_g[l], attn_sinks[l], batch=batch, seq=seq)
        y_c = _sgu(proj, sgu_ln_g[l], sgu_ln_b[l], sgu_w[l], sgu_b[l], tm=512)
        merged = _merge(y_a, y_b, y_c, w_branch, l, proj, tm=1024, tn=512)
        xt = _matmul(merged, w_out, l, residual=xt, tm=512, tn=D_MODEL)
        q = _matmul(xt, w_query, l, gain=norm2_g[l], tm=512, tn=D_MODEL)
        count, w1, rank2, w2 = _peer_select(q, peer_keys, l, tm=256)
        xt = _peer_ffn(xt, norm2_g[l], peer_u, peer_vt, l, count, w1, rank2, w2, tm=512, te=1024)
    return xt.reshape(batch, seq, D_MODEL)
```

```python
import functools
import math

import jax
import jax.numpy as jnp
from jax import lax
from jax.experimental import pallas as pl
from jax.experimental.pallas import tpu as pltpu

F32 = jnp.float32
BF16 = jnp.bfloat16

D_MODEL = 2048
DEPTH = 2
SSM_WIDTH = 1024
SSM_GROUP = 16
SSM_GROUPS = SSM_WIDTH // SSM_GROUP
SSM_STATE = 64
HEAD_DIM = 64
N_Q_HEADS = 16
N_KV_HEADS = 4
Q_PER_KV = N_Q_HEADS // N_KV_HEADS
WINDOW = 128
ATTN_WIDTH = N_Q_HEADS * HEAD_DIM
KV_WIDTH = N_KV_HEADS * HEAD_DIM
SGU_WIDTH = 1024
SGU_HEADS = 8
SGU_HEAD_DIM = SGU_WIDTH // SGU_HEADS
SGU_CHUNK = 128
N_BRANCH = 3
BRANCH_WIDTH = 1024
PEER_HEADS = 8
PEER_QDIM = 256
PEER_HALF = PEER_QDIM // 2
N_KEYS = 128
N_EXPERTS = N_KEYS * N_KEYS
PEER_TOPK = 16
EPS = 1e-6

V7X_VMEM_BYTES = 64 * 1024 * 1024
SUBLANES = 8
LANES = 128
BF16_ROWS = 2 * SUBLANES
MXU_TILE = 256

COL_SSM = 0
COL_Q = COL_SSM + SSM_WIDTH
COL_K = COL_Q + ATTN_WIDTH
COL_V = COL_K + KV_WIDTH
COL_SGU = COL_V + KV_WIDTH
COL_GATE = COL_SGU + 2 * SGU_WIDTH
IN_COLS = COL_GATE + N_BRANCH * D_MODEL

S5_GB = 16
S5_NGB = SSM_GROUPS // S5_GB
S5_IN = S5_GB * SSM_GROUP
S5_ST = S5_GB * SSM_STATE
S5_CHUNK = 512

SQRT_HALF = math.sqrt(0.5)


def _vmem_limit(pipelined_bytes, resident_bytes):
    need = 2 * pipelined_bytes + resident_bytes + (4 << 20)
    return int(min(need, V7X_VMEM_BYTES - (8 << 20)))


def _gelu(x):
    return 0.5 * x * (1.0 + lax.erf(x * SQRT_HALF))


def _rms(x, g):
    return x * lax.rsqrt(jnp.mean(x * x, axis=-1, keepdims=True) + EPS) * g


def _matmul_kernel(*refs, norm, residual):
    refs = list(refs)
    a_ref = refs.pop(0)
    g_ref = refs.pop(0) if norm else None
    w_ref = refs.pop(0)
    r_ref = refs.pop(0) if residual else None
    o_ref, abf_ref = refs

    @pl.when(pl.program_id(1) == 0)
    def _prepare_lhs():
        a = a_ref[...]
        if norm:
            a = _rms(a, g_ref[...])
        abf_ref[...] = a.astype(BF16)

    acc = jnp.dot(abf_ref[...], w_ref[...], preferred_element_type=F32)
    if residual:
        acc = r_ref[...] + acc
    o_ref[...] = acc


def _matmul(a, w, layer, *, gain=None, residual=None, tm, tn):
    m, k = a.shape
    n = w.shape[2]
    assert m % tm == 0 and n % tn == 0
    operands = [a]
    in_specs = [pl.BlockSpec((tm, k), lambda i, j: (i, 0))]
    if gain is not None:
        operands.append(gain.reshape(1, k))
        in_specs.append(pl.BlockSpec((1, k), lambda i, j: (0, 0)))
    operands.append(w)
    in_specs.append(pl.BlockSpec((None, k, tn), lambda i, j: (layer, 0, j)))
    if residual is not None:
        operands.append(residual)
        in_specs.append(pl.BlockSpec((tm, tn), lambda i, j: (i, j)))
    pipelined = tm * k * 4 + k * tn * 2 + tm * tn * 4 * (2 if residual is not None else 1)
    return pl.pallas_call(
        functools.partial(_matmul_kernel, norm=gain is not None, residual=residual is not None),
        grid=(m // tm, n // tn),
        in_specs=in_specs,
        out_specs=pl.BlockSpec((tm, tn), lambda i, j: (i, j)),
        out_shape=jax.ShapeDtypeStruct((m, n), F32),
        scratch_shapes=[pltpu.VMEM((tm, k), BF16)],
        compiler_params=pltpu.CompilerParams(
            dimension_semantics=("parallel", "arbitrary"),
            vmem_limit_bytes=_vmem_limit(pipelined, tm * k * 2 + tm * k * 4)),
        name="matmul",
    )(*operands)


def _s5_kernel(u_ref, d_ref, bm_ref, cm_ref, pw_ref, o_ref, x_ref, carry_ref):
    nb, tc = u_ref.shape[0], u_ref.shape[1]
    ng = tc // SUBLANES

    @pl.when(pl.program_id(1) == 0)
    def _reset_state():
        carry_ref[...] = jnp.zeros_like(carry_ref)

    for b in range(nb):
        bu = jnp.dot(u_ref[b].astype(BF16), bm_ref[0], preferred_element_type=F32)
        re = bu[:, :S5_ST].reshape(ng, SUBLANES, S5_ST)
        im = bu[:, S5_ST:].reshape(ng, SUBLANES, S5_ST)
        for idx, k in enumerate((1, 2, 4)):
            lre = pw_ref[0, 2 * idx]
            lim = pw_ref[0, 2 * idx + 1]
            sre = pltpu.roll(re, k, 1)
            sim = pltpu.roll(im, k, 1)
            re, im = re + (lre * sre - lim * sim), im + (lre * sim + lim * sre)
        x_ref[b, :, :S5_ST] = re.reshape(tc, S5_ST)
        x_ref[b, :, S5_ST:] = im.reshape(tc, S5_ST)

    cyre = pw_ref[0, 6]
    cyim = pw_ref[0, 7]

    def _carry_group(g, carry):
        r0 = pl.multiple_of(g * SUBLANES, SUBLANES)
        last = SUBLANES - 1
        out = []
        for b in range(nb):
            cr, ci = carry[2 * b], carry[2 * b + 1]
            xr = x_ref[b, pl.ds(r0, SUBLANES), :S5_ST] + (cyre * cr - cyim * ci)
            xi = x_ref[b, pl.ds(r0, SUBLANES), S5_ST:] + (cyre * ci + cyim * cr)
            x_ref[b, pl.ds(r0, SUBLANES), :S5_ST] = xr
            x_ref[b, pl.ds(r0, SUBLANES), S5_ST:] = xi
            out += [jnp.broadcast_to(xr[last:, :], (SUBLANES, S5_ST)),
                    jnp.broadcast_to(xi[last:, :], (SUBLANES, S5_ST))]
        return tuple(out)

    carry = lax.fori_loop(0, ng, _carry_group, tuple(carry_ref[k] for k in range(2 * nb)))
    for k in range(2 * nb):
        carry_ref[k] = carry[k]

    for b in range(nb):
        y = jnp.dot(x_ref[b].astype(BF16), cm_ref[0], preferred_element_type=F32)
        o_ref[b] = y + d_ref[...] * u_ref[b]


def _s5_discretise(a_re, a_im, b_re, b_im, c_re, c_im, log_dt):
    depth = a_re.shape[0]
    dt = jnp.exp(log_dt)[..., None]

    def a_pow(k):
        mag = jnp.exp(a_re * dt * k)
        return mag * jnp.cos(a_im * dt * k), mag * jnp.sin(a_im * dt * k)

    abar_re, abar_im = a_pow(1.0)
    num_re, num_im = abar_re - 1.0, abar_im
    den = a_re * a_re + a_im * a_im
    coef_re = (num_re * a_re + num_im * a_im) / den
    coef_im = (num_im * a_re - num_re * a_im) / den
    bbar_re = coef_re[..., None] * b_re - coef_im[..., None] * b_im
    bbar_im = coef_re[..., None] * b_im + coef_im[..., None] * b_re

    eye = jnp.eye(S5_GB, dtype=F32)

    def in_block(b):
        b = b.reshape(depth, S5_NGB, S5_GB, SSM_STATE, SSM_GROUP)
        return jnp.einsum('lbgph,gk->lbghkp', b, eye).reshape(depth, S5_NGB, S5_IN, S5_ST)

    def out_block(c):
        c = c.reshape(depth, S5_NGB, S5_GB, SSM_GROUP, SSM_STATE)
        return jnp.einsum('lbghp,gk->lbgpkh', c, eye).reshape(depth, S5_NGB, S5_ST, S5_IN)

    bmat = jnp.concatenate([in_block(bbar_re), in_block(bbar_im)], axis=3).astype(BF16)
    cmat = jnp.concatenate([out_block(c_re), out_block(-c_im)], axis=2).astype(BF16)

    rows = jnp.arange(SUBLANES, dtype=F32)[:, None, None, None]
    tiles = []
    for k in (1, 2, 4):
        pr, pi = a_pow(float(k))
        tiles += [jnp.where(rows >= k, pr[None], 0.0), jnp.where(rows >= k, pi[None], 0.0)]
    mag = jnp.exp(a_re[None] * dt[None] * (rows + 1.0))
    tiles += [mag * jnp.cos(a_im[None] * dt[None] * (rows + 1.0)), mag * jnp.sin(a_im[None] * dt[None] * (rows + 1.0))]
    pw = jnp.stack(tiles).reshape(len(tiles), SUBLANES, depth, S5_NGB, S5_ST).transpose(2, 3, 0, 1, 4)
    return bmat, cmat, pw


def _s5_scan(proj, ssm_d, bmat, cmat, pw, layer, *, batch, seq):
    nc = seq // S5_CHUNK
    proj = proj.reshape(batch, seq, proj.shape[1])
    pipelined = batch * S5_CHUNK * S5_IN * 4 * 2 + S5_IN * 2 * S5_ST * 2 * 2 + 8 * SUBLANES * S5_ST * 4
    resident = batch * S5_CHUNK * 2 * S5_ST * 4 * 4
    out = pl.pallas_call(
        _s5_kernel,
        grid=(S5_NGB, nc),
        in_specs=[
            pl.BlockSpec((batch, S5_CHUNK, S5_IN), lambda g, c: (0, c, COL_SSM // S5_IN + g)),
            pl.BlockSpec((None, 1, S5_IN), lambda g, c: (layer, 0, g)),
            pl.BlockSpec((None, 1, S5_IN, 2 * S5_ST), lambda g, c: (layer, g, 0, 0)),
            pl.BlockSpec((None, 1, 2 * S5_ST, S5_IN), lambda g, c: (layer, g, 0, 0)),
            pl.BlockSpec((None, 1, 8, SUBLANES, S5_ST), lambda g, c: (layer, g, 0, 0, 0)),
        ],
        out_specs=pl.BlockSpec((batch, S5_CHUNK, S5_IN), lambda g, c: (0, c, g)),
        out_shape=jax.ShapeDtypeStruct((batch, seq, SSM_WIDTH), F32),
        scratch_shapes=[pltpu.VMEM((batch, S5_CHUNK, 2 * S5_ST), F32),
                        pltpu.VMEM((2 * batch, SUBLANES, S5_ST), F32)],
        compiler_params=pltpu.CompilerParams(
            dimension_semantics=("parallel", "arbitrary"),
            vmem_limit_bytes=_vmem_limit(pipelined, resident)),
        name="s5_scan",
    )(proj, ssm_d[:, None, :], bmat, cmat, pw)
    return out.reshape(batch * seq, SSM_WIDTH)


def _s5_glu_kernel(y_ref, w_ref, b_ref, o_ref):
    y = _gelu(y_ref[...])
    z = jnp.dot(y.astype(BF16), w_ref[...], preferred_element_type=F32) + b_ref[...]
    o_ref[...] = (y * jax.nn.sigmoid(z)).astype(o_ref.dtype)


def _s5_glu(y, w_glu, layer, b_glu, *, tm):
    n = y.shape[0]
    return pl.pallas_call(
        _s5_glu_kernel,
        grid=(n // tm,),
        in_specs=[pl.BlockSpec((tm, SSM_WIDTH), lambda i: (i, 0)),
                  pl.BlockSpec((None, SSM_WIDTH, SSM_WIDTH), lambda i: (layer, 0, 0)),
                  pl.BlockSpec((1, SSM_WIDTH), lambda i: (0, 0))],
        out_specs=pl.BlockSpec((tm, SSM_WIDTH), lambda i: (i, 0)),
        out_shape=jax.ShapeDtypeStruct((n, SSM_WIDTH), BF16),
        compiler_params=pltpu.CompilerParams(
            dimension_semantics=("parallel",),
            vmem_limit_bytes=_vmem_limit(2 * tm * SSM_WIDTH * 4 + SSM_WIDTH * SSM_WIDTH * 2,
                                         4 * tm * SSM_WIDTH * 4)),
        name="s5_glu",
    )(y, w_glu, b_glu.reshape(1, SSM_WIDTH))


def _swa_kernel(sink_ref, q_ref, kp_ref, kc_ref, vp_ref, vc_ref, qg_ref, kg_ref, o_ref, *, blocks_per_seq):
    first = (pl.program_id(0) % blocks_per_seq) == 0
    q = q_ref[...]
    k = jnp.concatenate([kp_ref[...], kc_ref[...]], axis=0)
    v = jnp.concatenate([vp_ref[...], vc_ref[...]], axis=0)
    rows = Q_PER_KV * WINDOW
    qpos = (lax.broadcasted_iota(jnp.int32, (rows, 2 * WINDOW), 0) & (WINDOW - 1)) + WINDOW
    kpos = lax.broadcasted_iota(jnp.int32, (rows, 2 * WINDOW), 1)
    diff = qpos - kpos
    first_key = jnp.where(first, WINDOW, 0)
    valid = (diff >= 0) & (diff < WINDOW) & (kpos >= first_key)
    groups = [range(kv * Q_PER_KV, (kv + 1) * Q_PER_KV) for kv in range(N_KV_HEADS)]
    qb = jnp.stack([jnp.concatenate([q[:, h * HEAD_DIM:(h + 1) * HEAD_DIM] for h in grp], axis=0)
                    for grp in groups])
    kb = jnp.stack([k[:, kv * HEAD_DIM:(kv + 1) * HEAD_DIM] for kv in range(N_KV_HEADS)])
    vb = jnp.stack([v[:, kv * HEAD_DIM:(kv + 1) * HEAD_DIM] for kv in range(N_KV_HEADS)]).astype(BF16)
    qb = _rms(qb, qg_ref[...]).astype(BF16)
    kb = _rms(kb, kg_ref[...]).astype(BF16)
    sink = jnp.stack([jnp.concatenate([jnp.full((WINDOW, 1), sink_ref[h], F32) for h in grp], axis=0)
                      for grp in groups])
    s = jnp.einsum('gqd,gkd->gqk', qb, kb, preferred_element_type=F32)
    s = jnp.where(valid[None], s * (HEAD_DIM ** -0.5), -1e30)
    m = jnp.maximum(jnp.max(s, axis=-1, keepdims=True), sink)
    p = jnp.exp(s - m)
    denom = jnp.sum(p, axis=-1, keepdims=True) + jnp.exp(sink - m)
    o = jnp.einsum('gqk,gkd->gqd', p.astype(BF16), vb, preferred_element_type=F32) / denom
    outs = [o[kv, g * WINDOW:(g + 1) * WINDOW] for kv in range(N_KV_HEADS) for g in range(Q_PER_KV)]
    o_ref[...] = jnp.concatenate(outs, axis=-1).astype(o_ref.dtype)


def _swa(proj, q_g, k_g, sinks, *, batch, seq):
    n = batch * seq
    nb = seq // WINDOW
    prev = lambda r: jnp.maximum(r - 1, 0)
    pipelined = WINDOW * (2 * ATTN_WIDTH + 4 * KV_WIDTH) * 4
    return pl.pallas_call(
        functools.partial(_swa_kernel, blocks_per_seq=nb),
        grid=(n // WINDOW,),
        in_specs=[
            pl.BlockSpec(memory_space=pltpu.SMEM),
            pl.BlockSpec((WINDOW, ATTN_WIDTH), lambda r: (r, COL_Q // ATTN_WIDTH)),
            pl.BlockSpec((WINDOW, KV_WIDTH), lambda r: (prev(r), COL_K // KV_WIDTH)),
            pl.BlockSpec((WINDOW, KV_WIDTH), lambda r: (r, COL_K // KV_WIDTH)),
            pl.BlockSpec((WINDOW, KV_WIDTH), lambda r: (prev(r), COL_V // KV_WIDTH)),
            pl.BlockSpec((WINDOW, KV_WIDTH), lambda r: (r, COL_V // KV_WIDTH)),
            pl.BlockSpec((1, HEAD_DIM), lambda r: (0, 0)),
            pl.BlockSpec((1, HEAD_DIM), lambda r: (0, 0)),
        ],
        out_specs=pl.BlockSpec((WINDOW, ATTN_WIDTH), lambda r: (r, 0)),
        out_shape=jax.ShapeDtypeStruct((n, ATTN_WIDTH), BF16),
        compiler_params=pltpu.CompilerParams(
            dimension_semantics=("parallel",),
            vmem_limit_bytes=_vmem_limit(pipelined, 16 << 20)),
        name="swa",
    )(sinks, proj, proj, proj, proj, proj, q_g.reshape(1, HEAD_DIM), k_g.reshape(1, HEAD_DIM))


def _sgu_kernel(*refs):
    z_refs = refs[:SGU_PARTS]
    g_ref, b_ref, w_ref, bs_ref, o_ref = refs[SGU_PARTS:]
    tm = o_ref.shape[0]
    z = _gelu(jnp.concatenate([r[...] for r in z_refs], axis=1))
    u = z[:, :SGU_WIDTH]
    v = z[:, SGU_WIDTH:]
    vc = v - jnp.mean(v, axis=-1, keepdims=True)
    vn = vc * lax.rsqrt(jnp.mean(vc * vc, axis=-1, keepdims=True) + EPS) * g_ref[...] + b_ref[...]
    vn = vn.astype(BF16)
    row = lax.broadcasted_iota(jnp.int32, (SGU_CHUNK, SGU_CHUNK), 0)
    col = lax.broadcasted_iota(jnp.int32, (SGU_CHUNK, SGU_CHUNK), 1)
    for h in range(SGU_HEADS):
        w_h = jnp.where(row >= col, w_ref[h], 0.0).astype(BF16)
        cs = slice(h * SGU_HEAD_DIM, (h + 1) * SGU_HEAD_DIM)
        for c in range(tm // SGU_CHUNK):
            rs = slice(c * SGU_CHUNK, (c + 1) * SGU_CHUNK)
            mixed = jnp.dot(w_h, vn[rs, cs], preferred_element_type=F32) + bs_ref[:, h:h + 1]
            o_ref[rs, cs] = (u[rs, cs] * mixed).astype(o_ref.dtype)


SGU_PART = math.gcd(COL_SGU, 2 * SGU_WIDTH)
SGU_PARTS = 2 * SGU_WIDTH // SGU_PART


def _sgu(proj, ln_g, ln_b, w_s, b_s, *, tm):
    n = proj.shape[0]
    z_specs = [pl.BlockSpec((tm, SGU_PART), lambda i, p=p: (i, COL_SGU // SGU_PART + p)) for p in range(SGU_PARTS)]
    return pl.pallas_call(
        _sgu_kernel,
        grid=(n // tm,),
        in_specs=z_specs + [
            pl.BlockSpec((1, SGU_WIDTH), lambda i: (0, 0)),
            pl.BlockSpec((1, SGU_WIDTH), lambda i: (0, 0)),
            pl.BlockSpec((SGU_HEADS, SGU_CHUNK, SGU_CHUNK), lambda i: (0, 0, 0)),
            pl.BlockSpec((SGU_CHUNK, SGU_HEADS), lambda i: (0, 0)),
        ],
        out_specs=pl.BlockSpec((tm, SGU_WIDTH), lambda i: (i, 0)),
        out_shape=jax.ShapeDtypeStruct((n, SGU_WIDTH), BF16),
        compiler_params=pltpu.CompilerParams(
            dimension_semantics=("parallel",),
            vmem_limit_bytes=_vmem_limit(tm * 3 * SGU_WIDTH * 4, 6 * tm * 2 * SGU_WIDTH * 4)),
        name="sgu",
    )(*([proj] * SGU_PARTS), ln_g.reshape(1, SGU_WIDTH), ln_b.reshape(1, SGU_WIDTH), w_s, b_s.T)


def _merge_kernel(ya_ref, yb_ref, yc_ref, w_ref, ga_ref, gb_ref, gc_ref, o_ref):
    acc = None
    for n, (y_ref, gate_ref) in enumerate(((ya_ref, ga_ref), (yb_ref, gb_ref), (yc_ref, gc_ref))):
        branch = jnp.dot(y_ref[...], w_ref[n], preferred_element_type=F32)
        term = jax.nn.sigmoid(gate_ref[...]) * branch
        acc = term if acc is None else acc + term
    o_ref[...] = acc


def _merge(ya, yb, yc, w_branch, layer, proj, *, tm, tn):
    n = ya.shape[0]
    y_spec = pl.BlockSpec((tm, BRANCH_WIDTH), lambda i, j: (i, 0))

    def gate_spec(b):
        return pl.BlockSpec((tm, tn), lambda i, j: (i, (COL_GATE + b * D_MODEL) // tn + j))

    pipelined = 3 * tm * BRANCH_WIDTH * 2 + 3 * BRANCH_WIDTH * tn * 2 + 4 * tm * tn * 4
    return pl.pallas_call(
        _merge_kernel,
        grid=(n // tm, D_MODEL // tn),
        in_specs=[y_spec, y_spec, y_spec,
                  pl.BlockSpec((None, N_BRANCH, BRANCH_WIDTH, tn), lambda i, j: (layer, 0, 0, j)),
                  gate_spec(0), gate_spec(1), gate_spec(2)],
        out_specs=pl.BlockSpec((tm, tn), lambda i, j: (i, j)),
        out_shape=jax.ShapeDtypeStruct((n, D_MODEL), F32),
        compiler_params=pltpu.CompilerParams(
            dimension_semantics=("parallel", "parallel"),
            vmem_limit_bytes=_vmem_limit(pipelined, 4 * tm * tn * 4)),
        name="merge",
    )(ya, yb, yc, w_branch, proj, proj, proj)


def _top_values(s, count):
    rows = []
    rank = jnp.full(s.shape, float(count), F32)
    for r in range(count):
        m = jnp.max(s, axis=0, keepdims=True)
        rows.append(m)
        hit = s == m
        rank = jnp.where(hit, float(r), rank)
        s = jnp.where(hit, -jnp.inf, s)
    return jnp.concatenate(rows, axis=0), rank


def _peer_select_kernel(q_ref, keys_ref, n_ref, w1_ref, r2_ref, w2_ref):
    nt = lax.dot_general
    contract_last = (((1,), (1,)), ((), ()))
    for h in range(PEER_HEADS):
        c0 = h * PEER_QDIM
        q1 = q_ref[:, c0:c0 + PEER_HALF].astype(BF16)
        q2 = q_ref[:, c0 + PEER_HALF:c0 + PEER_QDIM].astype(BF16)
        s1 = nt(keys_ref[0, h], q1, contract_last, preferred_element_type=F32)
        s2 = nt(keys_ref[1, h], q2, contract_last, preferred_element_type=F32)
        a, _ = _top_values(s1, PEER_TOPK)
        b, rank2 = _top_values(s2, PEER_TOPK)
        cand = [a[0:1] + b]
        cand += [a[p:p + 1] + b[0:SUBLANES] for p in range(1, SUBLANES)]
        cand.append(a[SUBLANES:] + b[0:1])
        cand = jnp.concatenate(cand, axis=0)
        work = cand
        for _ in range(PEER_TOPK - 1):
            m = jnp.max(work, axis=0, keepdims=True)
            work = jnp.where(work == m, -jnp.inf, work)
        thr = jnp.max(work, axis=0, keepdims=True)
        top = a[0:1] + b[0:1]
        z = jnp.sum(jnp.where(cand >= thr, jnp.exp(cand - top), 0.0), axis=0, keepdims=True)
        count = jnp.zeros(s1.shape, F32)
        for r in range(PEER_TOPK):
            count = count + jnp.where(s1 + b[r:r + 1] >= thr, 1.0, 0.0)
        n_ref[h] = count[:, None, :]
        w1_ref[h] = jnp.exp(s1 - a[0:1])[:, None, :]
        packed = (N_KEYS // BF16_ROWS, BF16_ROWS, s2.shape[1])
        r2_ref[h] = rank2.astype(BF16).reshape(packed)
        w2_ref[h] = (jnp.exp(s2 - b[0:1]) / z).astype(BF16).reshape(packed)


def _peer_select(q, keys, layer, *, tm):
    n = q.shape[0]
    rows = jax.ShapeDtypeStruct((PEER_HEADS, N_KEYS, 1, n), F32)
    rows_spec = pl.BlockSpec((PEER_HEADS, N_KEYS, 1, tm), lambda i: (0, 0, 0, i))
    big = jax.ShapeDtypeStruct((PEER_HEADS, N_KEYS // BF16_ROWS, BF16_ROWS, n), BF16)
    big_spec = pl.BlockSpec((PEER_HEADS, N_KEYS // BF16_ROWS, BF16_ROWS, tm), lambda i: (0, 0, 0, i))
    return pl.pallas_call(
        _peer_select_kernel,
        grid=(n // tm,),
        in_specs=[pl.BlockSpec((tm, PEER_HEADS * PEER_QDIM), lambda i: (i, 0)),
                  pl.BlockSpec((None, 2, PEER_HEADS, N_KEYS, PEER_HALF), lambda i: (layer, 0, 0, 0, 0))],
        out_specs=[rows_spec, rows_spec, big_spec, big_spec],
        out_shape=[rows, rows, big, big],
        compiler_params=pltpu.CompilerParams(
            dimension_semantics=("parallel",),
            vmem_limit_bytes=_vmem_limit(tm * PEER_HEADS * PEER_QDIM * 4 + 4 * PEER_HEADS * N_KEYS * tm * 4,
                                         16 << 20)),
        name="peer_select",
    )(q, keys)


def _peer_stage(u_ref, vt_ref, n_ref, w1_ref, r2_ref, w2_ref, xn_ref, acc_ref, h_new, h_old):
    te, tm = u_ref.shape[0], xn_ref.shape[0]
    packed = (N_KEYS // BF16_ROWS, BF16_ROWS, LANES)

    def row(ref, hd, i, ls):
        return jnp.broadcast_to(ref[hd, i, :, ls], (BF16_ROWS, LANES)).astype(BF16)

    def gate_block(i, c):
        rs = slice(i * N_KEYS, (i + 1) * N_KEYS)
        ls = slice(c * LANES, (c + 1) * LANES)
        gate = None
        for hd in range(PEER_HEADS):
            margin = jnp.maximum(row(n_ref, hd, i, ls) - r2_ref[hd, :, :, ls], 0.0)
            term = jnp.minimum(margin, w2_ref[hd, :, :, ls]) * row(w1_ref, hd, i, ls)
            gate = term if gate is None else gate + term
        act = _gelu(h_old[rs, ls]).reshape(packed).astype(BF16) * gate
        return act.reshape(N_KEYS, LANES)

    def output_piece(t):
        cols = [jnp.concatenate([gate_block(i, c) for i in range(te // N_KEYS)], axis=0)
                for c in range(t // LANES, (t + MXU_TILE) // LANES)]
        a = jnp.concatenate(cols, axis=1)
        acc_ref[:, t:t + MXU_TILE] += jnp.dot(vt_ref[...], a, preferred_element_type=F32)

    def hidden_piece(r, rows):
        rs = slice(r, r + rows)
        h_new[rs, :] = lax.dot_general(u_ref[rs, :], xn_ref[...], (((1,), (1,)), ((), ())),
                                       preferred_element_type=F32)

    pieces = tm // MXU_TILE
    rows = te // pieces
    for k in range(pieces):
        if h_new is not None:
            hidden_piece(k * rows, rows)
        if h_old is not None:
            output_piece(k * MXU_TILE)


def _peer_ffn_kernel(x_ref, g_ref, u_ref, vt_ref, n_ref, w1_ref, r2_ref, w2_ref, o_ref,
                     xn_ref, acc_ref, h0_ref, h1_ref, *, tiles):
    s = pl.program_id(1)
    stage = functools.partial(_peer_stage, u_ref, vt_ref, n_ref, w1_ref, r2_ref, w2_ref, xn_ref, acc_ref)
    middle = (s > 0) & (s < tiles)

    @pl.when(s == 0)
    def _first_step():
        xn_ref[...] = _rms(x_ref[...], g_ref[...]).astype(BF16)
        acc_ref[...] = jnp.zeros_like(acc_ref)
        stage(h0_ref, None)

    @pl.when(middle & (s % 2 == 0))
    def _even_step():
        stage(h0_ref, h1_ref)

    @pl.when(middle & (s % 2 == 1))
    def _odd_step():
        stage(h1_ref, h0_ref)

    @pl.when(s == tiles)
    def _last_step():
        stage(None, h1_ref if tiles % 2 == 0 else h0_ref)
        o_ref[...] = x_ref[...] + acc_ref[...].T


PEER_PIPELINE_LAG = 1


def _peer_ffn(x, gain, u_bf, vt_bf, layer, count, w1, rank2, w2, *, tm, te):
    n = x.shape[0]
    slabs = te // N_KEYS
    tiles = N_EXPERTS // te
    clip = lambda s, lag: jnp.clip(s - lag, 0, tiles - 1)
    row_spec = pl.BlockSpec((PEER_HEADS, slabs, 1, tm), lambda t, s: (0, clip(s, PEER_PIPELINE_LAG), 0, t))
    full_spec = pl.BlockSpec((PEER_HEADS, N_KEYS // BF16_ROWS, BF16_ROWS, tm), lambda t, s: (0, 0, 0, t))
    pipelined = (2 * tm * D_MODEL * 4 + 2 * te * D_MODEL * 2
                 + 2 * PEER_HEADS * (slabs * SUBLANES * 4 + N_KEYS * 2) * tm)
    resident = tm * D_MODEL * 2 + D_MODEL * tm * 4 + 2 * te * tm * 4 + D_MODEL * tm * 4
    return pl.pallas_call(
        functools.partial(_peer_ffn_kernel, tiles=tiles),
        grid=(n // tm, tiles + PEER_PIPELINE_LAG),
        in_specs=[
            pl.BlockSpec((tm, D_MODEL), lambda t, s: (t, 0)),
            pl.BlockSpec((1, D_MODEL), lambda t, s: (0, 0)),
            pl.BlockSpec((None, te, D_MODEL), lambda t, s: (layer, clip(s, 0), 0)),
            pl.BlockSpec((None, D_MODEL, te), lambda t, s: (layer, 0, clip(s, PEER_PIPELINE_LAG))),
            row_spec, row_spec, full_spec, full_spec,
        ],
        out_specs=pl.BlockSpec((tm, D_MODEL), lambda t, s: (t, 0)),
        out_shape=jax.ShapeDtypeStruct((n, D_MODEL), F32),
        scratch_shapes=[pltpu.VMEM((tm, D_MODEL), BF16),
                        pltpu.VMEM((D_MODEL, tm), F32),
                        pltpu.VMEM((te, tm), F32), pltpu.VMEM((te, tm), F32)],
        compiler_params=pltpu.CompilerParams(
            dimension_semantics=("parallel", "arbitrary"),
            vmem_limit_bytes=_vmem_limit(pipelined, resident)),
        name="peer_ffn",
    )(x, gain.reshape(1, D_MODEL), u_bf, vt_bf, count, w1, rank2, w2)


IN_PROJ_TN = IN_COLS // 7


def kernel(x, norm1_g, w_in, ssm_a_re, ssm_a_im, ssm_b_re, ssm_b_im, ssm_c_re, ssm_c_im, ssm_d, ssm_log_dt,
           w_glu, b_glu, q_norm_g, k_norm_g, attn_sinks, sgu_ln_g, sgu_ln_b, sgu_w, sgu_b, w_branch, w_out,
           norm2_g, w_query, peer_keys, peer_u, peer_v):
    batch, seq, d_model = x.shape
    assert d_model == D_MODEL and seq % S5_CHUNK == 0
    xt = x.reshape(batch * seq, D_MODEL)
    w_in, w_glu, w_branch, w_out, w_query, peer_keys, peer_u = (
        w.astype(BF16) for w in (w_in, w_glu, w_branch, w_out, w_query, peer_keys, peer_u))
    peer_vt = jnp.swapaxes(peer_v, 1, 2).astype(BF16)
    bmat, cmat, pw = _s5_discretise(ssm_a_re, ssm_a_im, ssm_b_re, ssm_b_im, ssm_c_re, ssm_c_im, ssm_log_dt)
    for l in range(DEPTH):
        proj = _matmul(xt, w_in, l, gain=norm1_g[l], tm=1024, tn=IN_PROJ_TN)
        y_ssm = _s5_scan(proj, ssm_d, bmat, cmat, pw, l, batch=batch, seq=seq)
        y_a = _s5_glu(y_ssm, w_glu, l, b_glu[l], tm=512)
        y_b = _swa(proj, q_norm_g[l], k_norm_g[l], attn_sinks[l], batch=batch, seq=seq)
        y_c = _sgu(proj, sgu_ln_g[l], sgu_ln_b[l], sgu_w[l], sgu_b[l], tm=512)
        merged = _merge(y_a, y_b, y_c, w_branch, l, proj, tm=1024, tn=512)
        xt = _matmul(merged, w_out, l, residual=xt, tm=512, tn=D_MODEL)
        q = _matmul(xt, w_query, l, gain=norm2_g[l], tm=512, tn=D_MODEL)
        count, w1, rank2, w2 = _peer_select(q, peer_keys, l, tm=256)
        xt = _peer_ffn(xt, norm2_g[l], peer_u, peer_vt, l, count, w1, rank2, w2, tm=512, te=1024)
    return xt.reshape(batch, seq, D_MODEL)
```
